```python
import jax, jax.numpy as jnp
from jax import lax
import numpy as np

D_MODEL = 4096
BATCH = 4
SEQ = 4096
DEPTH = 4

CHUNK = 64
N_MIXERS = 2
N_GLA_LAYERS = (DEPTH + 1) // 2
N_ATT_LAYERS = DEPTH // 2
ALPHA = (2.0 * DEPTH) ** 0.25
BETA = (8.0 * DEPTH) ** -0.25
LN_EPS = 1e-5

GLA_HEADS = 4
GLA_DK = D_MODEL // 2
GLA_DV = D_MODEL
GLA_HK = GLA_DK // GLA_HEADS
GLA_HV = GLA_DV // GLA_HEADS
GLA_GATE_RANK = 16
GLA_TAU = 16.0

ATT_HEAD_DIM = 128
ATT_HEADS = D_MODEL // ATT_HEAD_DIM
LEFT_CHUNKS = 8
BAND = (LEFT_CHUNKS + 1) * CHUNK
REL_CLIP = 256

MEM_LEN = 256
MEM_HEADS = 4
MEM_HEAD_DIM = 256
MEM_DIM = MEM_HEADS * MEM_HEAD_DIM

N_EXPERTS = 16
N_GROUPS = 4
EXPERTS_PER_GROUP = N_EXPERTS // N_GROUPS
TOP_K = 2
EXPERT_FF = 512

kernel_name = "hybrid_gla_bandattn_memxattn_groupmoe_deepnorm"


def layer_norm(x, g, b):
    xf = x.astype(jnp.float32)
    mu = jnp.mean(xf, axis=-1, keepdims=True)
    var = jnp.mean(jnp.square(xf - mu), axis=-1, keepdims=True)
    y = (xf - mu) * lax.rsqrt(var + LN_EPS)
    return (y * g.astype(jnp.float32) + b.astype(jnp.float32)).astype(x.dtype)


def gla_mixer(x, w_in, w_gate_down, w_gate_up, b_gate, norm_g, w_o):
    B, S, _ = x.shape
    n = S // CHUNK
    z = x @ w_in
    q, k, v, g = jnp.split(z, [GLA_DK, 2 * GLA_DK, 2 * GLA_DK + GLA_DV], axis=-1)
    log_a = jax.nn.log_sigmoid(((x @ w_gate_down) @ w_gate_up + b_gate).astype(jnp.float32)) / GLA_TAU

    def to_chunks(t, hd):
        return t.reshape(B, n, CHUNK, GLA_HEADS, hd).transpose(1, 0, 3, 2, 4)

    qc = to_chunks(q * (GLA_HK ** -0.5), GLA_HK)
    kc = to_chunks(k, GLA_HK)
    vc = to_chunks(v, GLA_HV)
    ac = to_chunks(log_a, GLA_HK)
    causal = jnp.tril(jnp.ones((CHUNK, CHUNK), dtype=bool))

    def step(state, inp):
        qn, kn, vn, an = (t.astype(jnp.float32) for t in inp)
        b = jnp.cumsum(an, axis=2)
        b_last = b[:, :, -1:, :]
        q_e = qn * jnp.exp(b)
        k_e = kn * jnp.exp(-b)
        att = jnp.where(causal, jnp.einsum('bhik,bhjk->bhij', q_e, k_e), 0.0)
        o = jnp.einsum('bhij,bhjv->bhiv', att, vn) + jnp.einsum('bhik,bhkv->bhiv', q_e, state)
        k_dec = kn * jnp.exp(b_last - b)
        state = jnp.exp(b_last[:, :, 0, :])[..., None] * state + jnp.einsum('bhjk,bhjv->bhkv', k_dec, vn)
        return state, o

    s0 = jnp.zeros((B, GLA_HEADS, GLA_HK, GLA_HV), jnp.float32)
    _, o = lax.scan(step, s0, (qc, kc, vc, ac))
    o = o * lax.rsqrt(jnp.mean(jnp.square(o), axis=-1, keepdims=True) + LN_EPS)
    o = o.transpose(1, 0, 3, 2, 4).reshape(B, S, GLA_DV)
    o = (o * norm_g.astype(jnp.float32)).astype(x.dtype) * jax.nn.silu(g)
    return o @ w_o


def chunk_attention(x, w_qkv, rel_table, w_o):
    B, S, _ = x.shape
    n = S // CHUNK
    qkv = (x @ w_qkv).reshape(B, S, 3, ATT_HEADS, ATT_HEAD_DIM)
    q = qkv[:, :, 0].transpose(0, 2, 1, 3) * (ATT_HEAD_DIM ** -0.5)
    k = qkv[:, :, 1].transpose(0, 2, 1, 3)
    v = qkv[:, :, 2].transpose(0, 2, 1, 3)
    pad = LEFT_CHUNKS * CHUNK
    k_pad = jnp.pad(k, ((0, 0), (0, 0), (pad, 0), (0, 0)))
    v_pad = jnp.pad(v, ((0, 0), (0, 0), (pad, 0), (0, 0)))
    qi = jnp.arange(CHUNK)[:, None]
    kj = jnp.arange(BAND)[None, :]
    rel = qi + pad - kj
    rel_idx = jnp.clip(rel, -REL_CLIP, REL_CLIP) + REL_CLIP
    bias = rel_table[:, rel_idx].astype(jnp.float32)
    key_chunk = jnp.arange(BAND) // CHUNK

    def one_chunk(c):
        start = c * CHUNK
        qn = lax.dynamic_slice_in_dim(q, start, CHUNK, axis=2)
        kb = lax.dynamic_slice_in_dim(k_pad, start, BAND, axis=2)
        vb = lax.dynamic_slice_in_dim(v_pad, start, BAND, axis=2)
        s = jnp.einsum('bhid,bhjd->bhij', qn, kb).astype(jnp.float32) + bias
        valid = key_chunk >= (LEFT_CHUNKS - c)
        s = jnp.where(valid, s, -jnp.inf)
        p = jax.nn.softmax(s, axis=-1).astype(x.dtype)
        return jnp.einsum('bhij,bhjd->bhid', p, vb)

    o = lax.map(one_chunk, jnp.arange(n))
    o = o.transpose(1, 0, 3, 2, 4).reshape(B, S, D_MODEL)
    return o @ w_o


def memory_cross_attention(x, mem, w_q, w_kv, w_o):
    B, S, _ = x.shape
    M = mem.shape[1]
    q = (x @ w_q).reshape(B, S, MEM_HEADS, MEM_HEAD_DIM)
    kv = (mem @ w_kv).reshape(B, M, 2, MEM_HEADS, MEM_HEAD_DIM)
    s = jnp.einsum('bshd,bmhd->bhsm', q, kv[:, :, 0]).astype(jnp.float32) * (MEM_HEAD_DIM ** -0.5)
    p = jax.nn.softmax(s, axis=-1).astype(x.dtype)
    o = jnp.einsum('bhsm,bmhd->bshd', p, kv[:, :, 1]).reshape(B, S, MEM_DIM)
    return o @ w_o


def grouped_moe(x, router_w, router_b, w_gu, w_down):
    B, S, D = x.shape
    xt = x.reshape(-1, D)
    logits = (xt @ router_w).astype(jnp.float32) + router_b.astype(jnp.float32)
    probs = jax.nn.softmax(logits, axis=-1)
    grouped = probs.reshape(-1, N_GROUPS, EXPERTS_PER_GROUP)
    group_score = jnp.sum(lax.top_k(grouped, TOP_K)[0], axis=-1)
    best_group = jnp.argmax(group_score, axis=-1)
    in_group = (jnp.arange(N_EXPERTS) // EXPERTS_PER_GROUP)[None, :] == best_group[:, None]
    top_p, top_i = lax.top_k(jnp.where(in_group, probs, -1.0), TOP_K)
    top_p = top_p / jnp.sum(top_p, axis=-1, keepdims=True)
    combine = jnp.sum(jax.nn.one_hot(top_i, N_EXPERTS, dtype=jnp.float32) * top_p[..., None], axis=1)
    combine = combine.astype(x.dtype)
    y = jnp.zeros_like(xt)
    for e in range(N_EXPERTS):
        gate, up = jnp.split(xt @ w_gu[e], 2, axis=-1)
        y = y + combine[:, e:e + 1] * ((jax.nn.silu(gate) * up) @ w_down[e])
    return y.reshape(B, S, D)


def setup_inputs(seed: int = 0) -> dict:
    key = jax.random.key(seed)
    ks = jax.random.split(key, 24)

    def nrm(k, shape, scale):
        return jax.random.normal(k, shape, jnp.float32) * scale

    D = D_MODEL
    return {
        "x": nrm(ks[0], (BATCH, SEQ, D), 1.0),
        "mem": nrm(ks[1], (BATCH, MEM_LEN, D), 1.0),
        "gla_w_in": nrm(ks[2], (N_GLA_LAYERS, D, 2 * GLA_DK + 2 * GLA_DV), D ** -0.5),
        "gla_w_gate_down": nrm(ks[3], (N_GLA_LAYERS, D, GLA_GATE_RANK), D ** -0.5),
        "gla_w_gate_up": nrm(ks[4], (N_GLA_LAYERS, GLA_GATE_RANK, GLA_DK), GLA_GATE_RANK ** -0.5),
        "gla_b_gate": nrm(ks[5], (N_GLA_LAYERS, GLA_DK), 0.5),
        "gla_norm_g": 1.0 + nrm(ks[6], (N_GLA_LAYERS, GLA_DV), 0.02),
        "gla_w_o": nrm(ks[7], (N_GLA_LAYERS, GLA_DV, D), BETA * GLA_DV ** -0.5),
        "att_w_qkv": nrm(ks[8], (N_ATT_LAYERS, D, 3 * D), D ** -0.5),
        "att_rel_bias": nrm(ks[9], (N_ATT_LAYERS, ATT_HEADS, 2 * REL_CLIP + 1), 0.1),
        "att_w_o": nrm(ks[10], (N_ATT_LAYERS, D, D), BETA * D ** -0.5),
        "mx_w_q": nrm(ks[11], (DEPTH, D, MEM_DIM), D ** -0.5),
        "mx_w_kv": nrm(ks[12], (DEPTH, D, 2 * MEM_DIM), D ** -0.5),
        "mx_w_o": nrm(ks[13], (DEPTH, MEM_DIM, D), BETA * MEM_DIM ** -0.5),
        "router_w": nrm(ks[14], (D, N_EXPERTS), D ** -0.5),
        "router_b": nrm(ks[15], (N_EXPERTS,), 0.01),
        "moe_w_gu": nrm(ks[16], (DEPTH, N_EXPERTS, D, 2 * EXPERT_FF), D ** -0.5),
        "moe_w_down": nrm(ks[17], (DEPTH, N_EXPERTS, EXPERT_FF, D), BETA * EXPERT_FF ** -0.5),
        "ln_g": 1.0 + nrm(ks[18], (DEPTH, 3, D), 0.02),
        "ln_b": nrm(ks[19], (DEPTH, 3, D), 0.02),
    }


def reference(x, mem, gla_w_in, gla_w_gate_down, gla_w_gate_up, gla_b_gate, gla_norm_g, gla_w_o,
              att_w_qkv, att_rel_bias, att_w_o, mx_w_q, mx_w_kv, mx_w_o,
              router_w, router_b, moe_w_gu, moe_w_down, ln_g, ln_b):
    for i in range(DEPTH):
        j = i // N_MIXERS
        if i % N_MIXERS == 0:
            h = gla_mixer(x, gla_w_in[j], gla_w_gate_down[j], gla_w_gate_up[j],
                          gla_b_gate[j], gla_norm_g[j], gla_w_o[j])
        else:
            h = chunk_attention(x, att_w_qkv[j], att_rel_bias[j], att_w_o[j])
        x = layer_norm(ALPHA * x + h, ln_g[i, 0], ln_b[i, 0])
        h = memory_cross_attention(x, mem, mx_w_q[i], mx_w_kv[i], mx_w_o[i])
        x = layer_norm(ALPHA * x + h, ln_g[i, 1], ln_b[i, 1])
        h = grouped_moe(x, router_w, router_b, moe_w_gu[i], moe_w_down[i])
        x = layer_norm(ALPHA * x + h, ln_g[i, 2], ln_b[i, 2])
    return x
```

```python
import functools

import jax
import jax.numpy as jnp
import numpy as np
from jax import lax
from jax.experimental import pallas as pl
from jax.experimental.pallas import tpu as pltpu

F32 = jnp.float32
BF16 = jnp.bfloat16
U32 = jnp.uint32
I32 = jnp.int32

CHUNK = 64
LN_EPS = 1e-5
GLA_HEADS = 4
GLA_GATE_TAU = 16.0
ATT_HEAD_DIM = 128
ATT_LEFT_CHUNKS = 8
ATT_BAND = (ATT_LEFT_CHUNKS + 1) * CHUNK
ATT_REL_CLIP = 256
MEM_HEADS = 4
MOE_GROUPS = 4

LANES = 128
SUBLANES = 8
VMEM_BYTES = 64 * 1024 * 1024

ROW_SLAB = 16

_TRANS_B = (((1,), (1,)), ((), ()))
_TRANS_A = (((0,), (0,)), ((), ()))


def _cparams(sem, vmem_mb):
    return pltpu.CompilerParams(dimension_semantics=sem, vmem_limit_bytes=vmem_mb * 1024 * 1024)


def _tile(dim, pref):
    t = min(dim, pref)
    assert dim % t == 0, (dim, pref)
    return t


def _matmul_kernel(a_ref, w_ref, o_ref, acc_ref):
    k = pl.program_id(2)

    @pl.when(k == 0)
    def _():
        acc_ref[...] = jnp.zeros_like(acc_ref)

    acc_ref[...] += jnp.dot(a_ref[...], w_ref[...], preferred_element_type=F32)

    @pl.when(k == pl.num_programs(2) - 1)
    def _():
        o_ref[...] = acc_ref[...].astype(o_ref.dtype)


def _matmul_single_k_kernel(a_ref, w_ref, o_ref):
    o_ref[...] = jnp.dot(a_ref[...], w_ref[...], preferred_element_type=F32).astype(o_ref.dtype)


def matmul(a, w, out_dtype, tm=1024, tn=1024, tk=1024):
    m, k = a.shape
    k2, n = w.shape
    assert k == k2
    tm, tn, tk = _tile(m, tm), _tile(n, tn), _tile(k, tk)
    if k == tk:
        return pl.pallas_call(
            _matmul_single_k_kernel,
            grid=(m // tm, n // tn),
            in_specs=[pl.BlockSpec((tm, k), lambda i, j: (i, 0)), pl.BlockSpec((k, tn), lambda i, j: (0, j))],
            out_specs=pl.BlockSpec((tm, tn), lambda i, j: (i, j)),
            out_shape=jax.ShapeDtypeStruct((m, n), out_dtype),
            compiler_params=_cparams(("parallel", "parallel"), 40),
            name="matmul",
        )(a, w)
    return pl.pallas_call(
        _matmul_kernel,
        grid=(m // tm, n // tn, k // tk),
        in_specs=[
            pl.BlockSpec((tm, tk), lambda i, j, kk: (i, kk)),
            pl.BlockSpec((tk, tn), lambda i, j, kk: (kk, j)),
        ],
        out_specs=pl.BlockSpec((tm, tn), lambda i, j, kk: (i, j)),
        out_shape=jax.ShapeDtypeStruct((m, n), out_dtype),
        scratch_shapes=[pltpu.VMEM((tm, tn), F32)],
        compiler_params=_cparams(("parallel", "parallel", "arbitrary"), 40),
        name="matmul",
    )(a, w)


def _layer_norm(y, g, b):
    mu = jnp.mean(y, axis=-1, keepdims=True)
    yc = y - mu
    var = jnp.mean(yc * yc, axis=-1, keepdims=True)
    return yc * lax.rsqrt(var + LN_EPS) * g + b


def _round_bf16_bits(v):
    return lax.bitcast_convert_type(v.astype(BF16).astype(F32), U32)


def _pack_rows(yn, o_ref, rows):
    half = yn.shape[1] // 2
    packed = (_round_bf16_bits(yn[:, :half]) >> 16) | (_round_bf16_bits(yn[:, half:]) & jnp.uint32(0xFFFF0000))
    for s in range(ROW_SLAB):
        o_ref[pl.ds(s, rows, stride=ROW_SLAB), :] = packed[:, s * LANES:(s + 1) * LANES]


def _unpack_rows(ref, base, rows, dtype):
    lo, hi = [], []
    for s in range(ROW_SLAB):
        u = ref[pl.ds(base + s, rows, stride=ROW_SLAB), :]
        lo.append(lax.bitcast_convert_type(u << 16, F32).astype(dtype))
        hi.append(lax.bitcast_convert_type(u & jnp.uint32(0xFFFF0000), F32).astype(dtype))
    return jnp.concatenate(lo, axis=1), jnp.concatenate(hi, axis=1)


def _res_ln_kernel(x_ref, h_ref, g_ref, b_ref, of_ref, o2_ref, *, alpha, packed):
    yn = _layer_norm(alpha * x_ref[...] + h_ref[...].astype(F32), g_ref[...], b_ref[...])
    of_ref[...] = yn
    if packed:
        _pack_rows(yn, o2_ref, yn.shape[0])
    else:
        o2_ref[...] = yn.astype(BF16)


def residual_layer_norm(x, h, g, b, alpha, packed, tm=256):
    t, d = x.shape
    tm = _tile(t, tm)
    assert d == 2 * ROW_SLAB * LANES or not packed
    if packed:
        o2_shape = jax.ShapeDtypeStruct((t * ROW_SLAB, LANES), U32)
        o2_spec = pl.BlockSpec((tm * ROW_SLAB, LANES), lambda i: (i, 0))
    else:
        o2_shape = jax.ShapeDtypeStruct((t, d), BF16)
        o2_spec = pl.BlockSpec((tm, d), lambda i: (i, 0))
    row = pl.BlockSpec((tm, d), lambda i: (i, 0))
    vec = pl.BlockSpec((1, d), lambda i: (0, 0))
    return pl.pallas_call(
        functools.partial(_res_ln_kernel, alpha=alpha, packed=packed),
        grid=(t // tm,),
        in_specs=[row, row, vec, vec],
        out_specs=[row, o2_spec],
        out_shape=[jax.ShapeDtypeStruct((t, d), F32), o2_shape],
        compiler_params=_cparams(("parallel",), 48),
        name="residual_layer_norm",
    )(x, h, g.reshape(1, d), b.reshape(1, d))


def _gla_kernel(q_ref, k_ref, v_ref, g_ref, gd_ref, wu_ref, bg_ref, ng_ref, o_ref, st_ref, *, n_chunks, q_scale):
    @pl.when(pl.program_id(2) == 0)
    def _():
        st_ref[...] = jnp.zeros_like(st_ref)

    ri = lax.broadcasted_iota(I32, (CHUNK, CHUNK), 0)
    ci = lax.broadcasted_iota(I32, (CHUNK, CHUNK), 1)
    causal = ri >= ci
    tri = causal.astype(BF16)

    def chunk(c, carry):
        rows = pl.ds(pl.multiple_of(c * CHUNK, CHUNK), CHUNK)
        q = q_ref[rows, :].astype(F32) * q_scale
        k = k_ref[rows, :].astype(F32)
        v = v_ref[rows, :]
        gate = jnp.dot(gd_ref[rows, :].astype(BF16), wu_ref[...], preferred_element_type=F32) + bg_ref[...]
        la = (jnp.minimum(gate, 0.0) - jnp.log(1.0 + jnp.exp(-jnp.abs(gate)))) * (1.0 / GLA_GATE_TAU)
        hi = la.astype(BF16)
        r1 = la - hi.astype(F32)
        mid = r1.astype(BF16)
        lo = (r1 - mid.astype(F32)).astype(BF16)
        bc = (jnp.dot(tri, hi, preferred_element_type=F32) + jnp.dot(tri, mid, preferred_element_type=F32)
              + jnp.dot(tri, lo, preferred_element_type=F32))
        bl = bc[CHUNK - 1:CHUNK, :]
        q_e = (q * jnp.exp(bc)).astype(BF16)
        k_e = (k * jnp.exp(-bc)).astype(BF16)
        k_d = (k * jnp.exp(bl - bc)).astype(BF16)
        att = lax.dot_general(q_e, k_e, _TRANS_B, preferred_element_type=F32)
        att = jnp.where(causal, att, 0.0).astype(BF16)
        st = st_ref[...]
        o = jnp.dot(att, v, preferred_element_type=F32) + lax.dot_general(
            q_e, st.astype(BF16), _TRANS_B, preferred_element_type=F32)
        st_ref[...] = st * jnp.exp(bl) + lax.dot_general(v, k_d, _TRANS_A, preferred_element_type=F32)
        on = o * lax.rsqrt(jnp.mean(o * o, axis=-1, keepdims=True) + LN_EPS) * ng_ref[...]
        g = g_ref[rows, :].astype(F32)
        o_ref[rows, :] = (on * (g / (1.0 + jnp.exp(-g)))).astype(o_ref.dtype)
        return carry

    lax.fori_loop(0, n_chunks, chunk, 0)


def gla_core(z, gd, wu, bg, ng, rows_per_step=512):
    b, s, zc = z.shape
    hk = wu.shape[2]
    hv = ng.shape[2]
    assert zc == GLA_HEADS * (2 * hk + 2 * hv) and hv == 2 * hk
    rb = _tile(s, rows_per_step)
    assert rb % CHUNK == 0
    kq = GLA_HEADS * hk // hk
    kv = 2 * GLA_HEADS * hk // hv
    kg = kv + GLA_HEADS
    return pl.pallas_call(
        functools.partial(_gla_kernel, n_chunks=rb // CHUNK, q_scale=float(hk) ** -0.5),
        grid=(b, GLA_HEADS, s // rb),
        in_specs=[
            pl.BlockSpec((None, rb, hk), lambda bi, h, r: (bi, r, h)),
            pl.BlockSpec((None, rb, hk), lambda bi, h, r: (bi, r, kq + h)),
            pl.BlockSpec((None, rb, hv), lambda bi, h, r: (bi, r, kv + h)),
            pl.BlockSpec((None, rb, hv), lambda bi, h, r: (bi, r, kg + h)),
            pl.BlockSpec((None, rb, LANES), lambda bi, h, r: (bi, r, 0)),
            pl.BlockSpec((None, LANES, hk), lambda bi, h, r: (h, 0, 0)),
            pl.BlockSpec((None, 1, hk), lambda bi, h, r: (h, 0, 0)),
            pl.BlockSpec((None, 1, hv), lambda bi, h, r: (h, 0, 0)),
        ],
        out_specs=pl.BlockSpec((None, rb, hv), lambda bi, h, r: (bi, r, h)),
        out_shape=jax.ShapeDtypeStruct((b, s, GLA_HEADS * hv), BF16),
        scratch_shapes=[pltpu.VMEM((hv, hk), F32)],
        compiler_params=_cparams(("parallel", "parallel", "arbitrary"), 40),
        name="gla_core",
    )(z, z, z, z, gd, wu, bg, ng)


def _band_attn_kernel(q_ref, k_ref, v_ref, bias_ref, o_ref, kp_ref, vp_ref, *, n_chunks, scale):
    pad = ATT_LEFT_CHUNKS * CHUNK
    s = n_chunks * CHUNK
    kp_ref[pl.ds(0, pad), :] = jnp.zeros((pad, ATT_HEAD_DIM), kp_ref.dtype)
    vp_ref[pl.ds(0, pad), :] = jnp.zeros((pad, ATT_HEAD_DIM), vp_ref.dtype)
    kp_ref[pl.ds(pad, s), :] = k_ref[...]
    vp_ref[pl.ds(pad, s), :] = v_ref[...]
    slot = lax.broadcasted_iota(I32, (CHUNK, ATT_BAND), 1)

    def chunk(c, carry):
        start = pl.multiple_of(c * CHUNK, CHUNK)
        q = q_ref[pl.ds(start, CHUNK), :]
        kb = kp_ref[pl.ds(start, ATT_BAND), :]
        vb = vp_ref[pl.ds(start, ATT_BAND), :]
        sc = lax.dot_general(q, kb, _TRANS_B, preferred_element_type=F32) * scale + bias_ref[...]
        sc = jnp.where(slot >= (ATT_LEFT_CHUNKS - c) * CHUNK, sc, -jnp.inf)
        p = jnp.exp(sc - jnp.max(sc, axis=-1, keepdims=True))
        den = jnp.sum(p, axis=-1, keepdims=True)
        o = jnp.dot(p.astype(BF16), vb, preferred_element_type=F32) / den
        o_ref[pl.ds(start, CHUNK), :] = o.astype(o_ref.dtype)
        return carry

    lax.fori_loop(0, n_chunks, chunk, 0)


def band_attention_core(qkv, bias):
    b, s, c3 = qkv.shape
    h = bias.shape[0]
    assert c3 == 3 * h * ATT_HEAD_DIM and s % CHUNK == 0
    blk = lambda off: pl.BlockSpec((None, s, ATT_HEAD_DIM), lambda bi, hi: (bi, 0, off + hi))
    return pl.pallas_call(
        functools.partial(_band_attn_kernel, n_chunks=s // CHUNK, scale=float(ATT_HEAD_DIM) ** -0.5),
        grid=(b, h),
        in_specs=[blk(0), blk(h), blk(2 * h), pl.BlockSpec((None, CHUNK, ATT_BAND), lambda bi, hi: (hi, 0, 0))],
        out_specs=blk(0),
        out_shape=jax.ShapeDtypeStruct((b, s, h * ATT_HEAD_DIM), BF16),
        scratch_shapes=[pltpu.VMEM((s + ATT_LEFT_CHUNKS * CHUNK, ATT_HEAD_DIM), BF16)] * 2,
        compiler_params=_cparams(("parallel", "parallel"), 40),
        name="band_attention_core",
    )(qkv, qkv, qkv, bias)


def _rel_bias(rel_table):
    qi = np.arange(CHUNK)[:, None]
    kj = np.arange(ATT_BAND)[None, :]
    rel_idx = np.clip(qi + ATT_LEFT_CHUNKS * CHUNK - kj, -ATT_REL_CLIP, ATT_REL_CLIP) + ATT_REL_CLIP
    return rel_table[:, rel_idx].astype(F32)


def _mem_attn_kernel(q_ref, kv_ref, o_ref, *, head_dim, scale):
    width = MEM_HEADS * head_dim
    for h in range(MEM_HEADS):
        cols = slice(h * head_dim, (h + 1) * head_dim)
        sc = lax.dot_general(q_ref[:, cols], kv_ref[:, cols], _TRANS_B, preferred_element_type=F32) * scale
        p = jnp.exp(sc - jnp.max(sc, axis=-1, keepdims=True))
        den = jnp.sum(p, axis=-1, keepdims=True)
        vh = kv_ref[:, width + h * head_dim:width + (h + 1) * head_dim]
        o_ref[:, cols] = (jnp.dot(p.astype(BF16), vh, preferred_element_type=F32) / den).astype(o_ref.dtype)


def memory_attention_core(q, kv, tq=512):
    b, s, width = q.shape
    m = kv.shape[1]
    head_dim = width // MEM_HEADS
    tq = _tile(s, tq)
    return pl.pallas_call(
        functools.partial(_mem_attn_kernel, head_dim=head_dim, scale=float(head_dim) ** -0.5),
        grid=(b, s // tq),
        in_specs=[
            pl.BlockSpec((None, tq, width), lambda bi, r: (bi, r, 0)),
            pl.BlockSpec((None, m, 2 * width), lambda bi, r: (bi, 0, 0)),
        ],
        out_specs=pl.BlockSpec((None, tq, width), lambda bi, r: (bi, r, 0)),
        out_shape=jax.ShapeDtypeStruct((b, s, width), BF16),
        compiler_params=_cparams(("parallel", "parallel"), 32),
        name="memory_attention_core",
    )(q, kv)


def _top2_of4(a, b, c, d):
    hi1, lo1 = jnp.maximum(a, b), jnp.minimum(a, b)
    hi2, lo2 = jnp.maximum(c, d), jnp.minimum(c, d)
    return jnp.maximum(hi1, hi2) + jnp.maximum(jnp.minimum(hi1, hi2), jnp.maximum(lo1, lo2))


def _argmax_first(vals):
    best, idx = vals[0], jnp.zeros(vals[0].shape, I32)
    for j in range(1, len(vals)):
        better = vals[j] > best
        best = jnp.where(better, vals[j], best)
        idx = jnp.where(better, j, idx)
    return best, idx


def _router_kernel(x_ref, wr_ref, br_ref, route_ref, wts_ref, cnt_ref, upper_ref, *, n_experts):
    tr = x_ref.shape[0]
    per_group = n_experts // MOE_GROUPS

    @pl.when(pl.program_id(0) == 0)
    def _():
        cnt_ref[...] = jnp.zeros_like(cnt_ref)
        earlier = lax.broadcasted_iota(I32, (tr, tr), 0) < lax.broadcasted_iota(I32, (tr, tr), 1)
        upper_ref[...] = earlier.astype(BF16)

    logits = jnp.dot(x_ref[...], wr_ref[...], preferred_element_type=F32, precision=lax.Precision.HIGHEST)
    lt = logits.T[:n_experts, :] + br_ref[:n_experts, :]
    ex = jnp.exp(lt - jnp.max(lt, axis=0, keepdims=True))
    probs = ex / jnp.sum(ex, axis=0, keepdims=True)
    p = [probs[e:e + 1, :] for e in range(n_experts)]

    assert per_group == 4
    scores = [_top2_of4(*p[per_group * g:per_group * (g + 1)]) for g in range(MOE_GROUPS)]
    _, best_group = _argmax_first(scores)
    in_group = [sum(jnp.where(best_group == g, p[per_group * g + j], 0.0) for g in range(MOE_GROUPS))
                for j in range(per_group)]
    p1, j1 = _argmax_first(in_group)
    p2, j2 = _argmax_first([jnp.where(j1 == j, -1.0, in_group[j]) for j in range(per_group)])
    e1 = best_group * per_group + j1
    e2 = best_group * per_group + j2
    w1 = p1 / (p1 + p2)
    w2 = p2 / (p1 + p2)

    erow = lax.broadcasted_iota(I32, (n_experts, tr), 0)
    hit1 = erow == e1
    hit2 = erow == e2
    onehot = hit1.astype(F32) + hit2.astype(F32)
    before = cnt_ref[:, 0:1] + jnp.dot(onehot.astype(BF16), upper_ref[...], preferred_element_type=F32)
    rank1 = jnp.sum(jnp.where(hit1, before, 0.0), axis=0, keepdims=True).astype(I32)
    rank2 = jnp.sum(jnp.where(hit2, before, 0.0), axis=0, keepdims=True).astype(I32)
    cnt_ref[...] += jnp.sum(onehot, axis=1, keepdims=True)

    rrow = lax.broadcasted_iota(I32, (SUBLANES, tr), 0)
    route_ref[...] = jnp.where(rrow == 0, e1, jnp.where(rrow == 1, e2, jnp.where(rrow == 2, rank1, rank2)))
    wrow = lax.broadcasted_iota(I32, (LANES, tr), 0)
    wts_ref[...] = jnp.where(wrow == 0, w1, jnp.where(wrow == 1, w2, 0.0)).T


def moe_router(x, wr, br, n_experts, tr=1024):
    t, d = x.shape
    tr = _tile(t, tr)
    return pl.pallas_call(
        functools.partial(_router_kernel, n_experts=n_experts),
        grid=(t // tr,),
        in_specs=[
            pl.BlockSpec((tr, d), lambda i: (i, 0)),
            pl.BlockSpec((d, LANES), lambda i: (0, 0)),
            pl.BlockSpec((LANES, 1), lambda i: (0, 0)),
        ],
        out_specs=[
            pl.BlockSpec((SUBLANES, tr), lambda i: (0, i)),
            pl.BlockSpec((tr, LANES), lambda i: (i, 0)),
            pl.BlockSpec((n_experts, LANES), lambda i: (0, 0)),
        ],
        out_shape=[
            jax.ShapeDtypeStruct((SUBLANES, t), I32),
            jax.ShapeDtypeStruct((t, LANES), F32),
            jax.ShapeDtypeStruct((n_experts, LANES), F32),
        ],
        scratch_shapes=[pltpu.VMEM((tr, tr), BF16)],
        compiler_params=_cparams(("arbitrary",), 56),
        name="moe_router",
    )(x, wr, br)


def _row_copy(src_ref, src_row, dst_ref, dst_row, sem):
    return pltpu.make_async_copy(
        src_ref.at[pl.ds(pl.multiple_of(src_row * ROW_SLAB, ROW_SLAB), ROW_SLAB), :],
        dst_ref.at[pl.ds(pl.multiple_of(dst_row * ROW_SLAB, ROW_SLAB), ROW_SLAB), :],
        sem)


def _rows_wait(src_ref, dst_ref, n_rows, sem):
    span = pl.ds(0, n_rows * ROW_SLAB)
    pltpu.make_async_copy(src_ref.at[span, :], dst_ref.at[span, :], sem).wait()


def _dispatch_kernel(pos_ref, xg_ref, init_ref, xs_ref, sem, *, td):
    del init_ref
    base = pl.program_id(0) * td

    def issue(t, carry):
        for slot in range(2):
            _row_copy(xg_ref, base + t, xs_ref, pos_ref[0, 0, slot * td + t], sem).start()
        return carry

    lax.fori_loop(0, td, issue, 0)
    _rows_wait(xg_ref, xs_ref, 2 * td, sem)


def moe_dispatch(xg, pos_blocks, n_sorted_rows, td):
    t = xg.shape[0] // ROW_SLAB
    init = jnp.zeros((n_sorted_rows * ROW_SLAB, LANES), U32)
    return pl.pallas_call(
        functools.partial(_dispatch_kernel, td=td),
        grid=(t // td,),
        in_specs=[
            pl.BlockSpec((1, 1, 2 * td), lambda i: (i, 0, 0), memory_space=pltpu.SMEM),
            pl.BlockSpec(memory_space=pl.ANY),
            pl.BlockSpec(memory_space=pl.ANY),
        ],
        out_specs=pl.BlockSpec(memory_space=pl.ANY),
        out_shape=jax.ShapeDtypeStruct(init.shape, U32),
        scratch_shapes=[pltpu.SemaphoreType.DMA],
        input_output_aliases={2: 0},
        compiler_params=_cparams(("arbitrary",), 16),
        name="moe_dispatch",
    )(pos_blocks, xg, init)


def _expert_kernel(te_ref, nu_ref, xs_ref, wgu_ref, wd_ref, y_ref, *, tm):
    del te_ref
    i = pl.program_id(0)

    @pl.when(i < nu_ref[0])
    def _():
        half = wgu_ref.shape[0] // 2
        ff = wd_ref.shape[0]
        x_lo, x_hi = _unpack_rows(xs_ref, 0, tm, BF16)
        gu = (jnp.dot(x_lo, wgu_ref[:half, :], preferred_element_type=F32)
              + jnp.dot(x_hi, wgu_ref[half:, :], preferred_element_type=F32))
        gate, up = gu[:, :ff], gu[:, ff:]
        hid = (gate / (1.0 + jnp.exp(-gate)) * up).astype(BF16)
        _pack_rows(jnp.dot(hid, wd_ref[...], preferred_element_type=F32), y_ref, tm)

    @pl.when(i >= nu_ref[0])
    def _():
        y_ref[...] = jnp.zeros_like(y_ref)


def moe_experts(xs, wgu, wd, tile_expert, n_used, tm):
    n_tiles = xs.shape[0] // (tm * ROW_SLAB)
    _, d, ff2 = wgu.shape
    rows = pl.BlockSpec((tm * ROW_SLAB, LANES), lambda i, te, nu: (i, 0))
    return pl.pallas_call(
        functools.partial(_expert_kernel, tm=tm),
        grid_spec=pltpu.PrefetchScalarGridSpec(
            num_scalar_prefetch=2,
            grid=(n_tiles,),
            in_specs=[
                rows,
                pl.BlockSpec((None, d, ff2), lambda i, te, nu: (te[i], 0, 0)),
                pl.BlockSpec((None, ff2 // 2, d), lambda i, te, nu: (te[i], 0, 0)),
            ],
            out_specs=rows,
        ),
        out_shape=jax.ShapeDtypeStruct(xs.shape, U32),
        compiler_params=_cparams(("arbitrary",), 56),
        name="moe_experts",
    )(tile_expert, n_used, xs, wgu, wd)


def _combine_ln_kernel(pos_ref, wts_ref, x_ref, ys_ref, g_ref, b_ref, of_ref, ob_ref, buf_ref, sem, *, tc, alpha):
    def issue(t, carry):
        for slot in range(2):
            _row_copy(ys_ref, pos_ref[0, 0, slot * tc + t], buf_ref, slot * tc + t, sem).start()
        return carry

    lax.fori_loop(0, tc, issue, 0)
    _rows_wait(ys_ref, buf_ref, 2 * tc, sem)
    y = alpha * x_ref[...]
    for slot in range(2):
        lo, hi = _unpack_rows(buf_ref, slot * tc * ROW_SLAB, tc, F32)
        y = y + wts_ref[:, slot:slot + 1] * jnp.concatenate([lo, hi], axis=1)
    yn = _layer_norm(y, g_ref[...], b_ref[...])
    of_ref[...] = yn
    ob_ref[...] = yn.astype(BF16)


def moe_combine_layer_norm(x, ys, pos_blocks, wts, g, b, alpha, tc):
    t, d = x.shape
    row = pl.BlockSpec((tc, d), lambda i: (i, 0))
    vec = pl.BlockSpec((1, d), lambda i: (0, 0))
    return pl.pallas_call(
        functools.partial(_combine_ln_kernel, tc=tc, alpha=alpha),
        grid=(t // tc,),
        in_specs=[
            pl.BlockSpec((1, 1, 2 * tc), lambda i: (i, 0, 0), memory_space=pltpu.SMEM),
            pl.BlockSpec((tc, LANES), lambda i: (i, 0)),
            row,
            pl.BlockSpec(memory_space=pl.ANY),
            vec,
            vec,
        ],
        out_specs=[row, row],
        out_shape=[jax.ShapeDtypeStruct((t, d), F32), jax.ShapeDtypeStruct((t, d), BF16)],
        scratch_shapes=[pltpu.VMEM((2 * tc * ROW_SLAB, LANES), U32), pltpu.SemaphoreType.DMA],
        compiler_params=_cparams(("arbitrary",), 48),
        name="moe_combine_layer_norm",
    )(pos_blocks, wts, x, ys, g.reshape(1, d), b.reshape(1, d))


def _pos_blocks(pos, tile):
    t = pos.shape[1]
    return pos.reshape(2, t // tile, tile).transpose(1, 0, 2).reshape(t // tile, 1, 2 * tile)


def grouped_moe_layer_norm(xf, xg, wr, br, wgu, wd, g, b, alpha, tm=256, td=512, tc=256):
    t, d = xf.shape
    n_experts = wgu.shape[0]
    route, wts, counts = moe_router(xf, wr, br, n_experts)
    cnt = counts[:, 0].astype(I32)
    padded = (cnt + tm - 1) // tm * tm
    ends = jnp.cumsum(padded)
    offs = ends - padded
    pos = offs[route[0:2]] + route[2:4]
    n_tiles = (2 * t) // tm + n_experts
    tile_expert = jnp.minimum(
        jnp.searchsorted(ends, jnp.arange(n_tiles, dtype=I32) * tm, side="right"), n_experts - 1).astype(I32)
    n_used = (ends[-1:] // tm).astype(I32)
    xs = moe_dispatch(xg, _pos_blocks(pos, td), n_tiles * tm, td)
    ys = moe_experts(xs, wgu, wd, tile_expert, n_used, tm)
    return moe_combine_layer_norm(xf, ys, _pos_blocks(pos, tc), wts, g, b, alpha, tc)


def kernel(x, mem, gla_w_in, gla_w_gate_down, gla_w_gate_up, gla_b_gate, gla_norm_g, gla_w_o, att_w_qkv, att_rel_bias, att_w_o, mx_w_q, mx_w_kv, mx_w_o, router_w, router_b, moe_w_gu, moe_w_down, ln_g, ln_b):
    bsz, seq, d = x.shape
    t = bsz * seq
    depth = ln_g.shape[0]
    alpha = (2.0 * depth) ** 0.25
    n_experts = router_w.shape[1]
    hk = gla_w_gate_up.shape[2] // GLA_HEADS
    hv = gla_norm_g.shape[1] // GLA_HEADS
    rank = gla_w_gate_down.shape[2]
    mem_len = mem.shape[1]

    xf = x.reshape(t, d)
    xb = xf.astype(BF16)
    mem_b = mem.reshape(bsz * mem_len, d).astype(BF16)
    wr = jnp.pad(router_w, ((0, 0), (0, LANES - n_experts)))
    br = jnp.pad(router_b, (0, LANES - n_experts)).reshape(LANES, 1)

    for i in range(depth):
        j = i // 2
        if i % 2 == 0:
            z = matmul(xb, gla_w_in[j].astype(BF16), BF16)
            wd_pad = jnp.pad(gla_w_gate_down[j], ((0, 0), (0, LANES - rank))).astype(BF16)
            gd = matmul(xb, wd_pad, F32)
            wu = jnp.pad(gla_w_gate_up[j].reshape(rank, GLA_HEADS, hk).transpose(1, 0, 2),
                         ((0, 0), (0, LANES - rank), (0, 0))).astype(BF16)
            o = gla_core(z.reshape(bsz, seq, -1), gd.reshape(bsz, seq, LANES), wu,
                         gla_b_gate[j].reshape(GLA_HEADS, 1, hk), gla_norm_g[j].reshape(GLA_HEADS, 1, hv))
            h = matmul(o.reshape(t, -1), gla_w_o[j].astype(BF16), BF16)
        else:
            qkv = matmul(xb, att_w_qkv[j].astype(BF16), BF16)
            o = band_attention_core(qkv.reshape(bsz, seq, -1), _rel_bias(att_rel_bias[j]))
            h = matmul(o.reshape(t, -1), att_w_o[j].astype(BF16), BF16)
        xf, xb = residual_layer_norm(xf, h, ln_g[i, 0], ln_b[i, 0], alpha, packed=False)

        q = matmul(xb, mx_w_q[i].astype(BF16), BF16)
        kv = matmul(mem_b, mx_w_kv[i].astype(BF16), BF16)
        o = memory_attention_core(q.reshape(bsz, seq, -1), kv.reshape(bsz, mem_len, -1))
        h = matmul(o.reshape(t, -1), mx_w_o[i].astype(BF16), BF16)
        xf, xg = residual_layer_norm(xf, h, ln_g[i, 1], ln_b[i, 1], alpha, packed=True)

        xf, xb = grouped_moe_layer_norm(xf, xg, wr, br, moe_w_gu[i].astype(BF16), moe_w_down[i].astype(BF16),
                                        ln_g[i, 2], ln_b[i, 2], alpha)
    return xf.reshape(bsz, seq, d)
```

```python
import functools

import jax
import jax.numpy as jnp
import numpy as np
from jax import lax
from jax.experimental import pallas as pl
from jax.experimental.pallas import tpu as pltpu

F32 = jnp.float32
BF16 = jnp.bfloat16
U32 = jnp.uint32
I32 = jnp.int32

CHUNK = 64
LN_EPS = 1e-5
GLA_HEADS = 4
GLA_GATE_TAU = 16.0
ATT_HEAD_DIM = 128
ATT_LEFT_CHUNKS = 8
ATT_BAND = (ATT_LEFT_CHUNKS + 1) * CHUNK
ATT_REL_CLIP = 256
ATT_QUERY_CHUNKS = 4
ATT_QB = ATT_QUERY_CHUNKS * CHUNK
ATT_WIN = (ATT_LEFT_CHUNKS + ATT_QUERY_CHUNKS) * CHUNK
MEM_HEADS = 4
MOE_GROUPS = 4

LANES = 128
SUBLANES = 8

ROW_SLAB = 16

_TRANS_B = (((1,), (1,)), ((), ()))
_TRANS_A = (((0,), (0,)), ((), ()))


def _cparams(sem, vmem_mb):
    return pltpu.CompilerParams(dimension_semantics=sem, vmem_limit_bytes=vmem_mb * 1024 * 1024)


def _tile(dim, pref):
    t = min(dim, pref)
    assert dim % t == 0, (dim, pref)
    return t


def _matmul_kernel(a_ref, w_ref, o_ref, acc_ref):
    k = pl.program_id(2)

    @pl.when(k == 0)
    def _():
        acc_ref[...] = jnp.zeros_like(acc_ref)

    acc_ref[...] += jnp.dot(a_ref[...], w_ref[...], preferred_element_type=F32)

    @pl.when(k == pl.num_programs(2) - 1)
    def _():
        o_ref[...] = acc_ref[...].astype(o_ref.dtype)


def _matmul_single_k_kernel(a_ref, w_ref, o_ref):
    o_ref[...] = jnp.dot(a_ref[...], w_ref[...], preferred_element_type=F32).astype(o_ref.dtype)


def matmul(a, w, layer, out_dtype, tm=1024, tn=1024, tk=1024):
    m, k = a.shape
    _, k2, n = w.shape
    assert k == k2
    tm, tn, tk = _tile(m, tm), _tile(n, tn), _tile(k, tk)
    if k == tk:
        return pl.pallas_call(
            _matmul_single_k_kernel,
            grid=(m // tm, n // tn),
            in_specs=[pl.BlockSpec((tm, k), lambda i, j: (i, 0)), pl.BlockSpec((None, k, tn), lambda i, j: (layer, 0, j))],
            out_specs=pl.BlockSpec((tm, tn), lambda i, j: (i, j)),
            out_shape=jax.ShapeDtypeStruct((m, n), out_dtype),
            compiler_params=_cparams(("parallel", "parallel"), 40),
            name="matmul",
        )(a, w)
    return pl.pallas_call(
        _matmul_kernel,
        grid=(m // tm, n // tn, k // tk),
        in_specs=[
            pl.BlockSpec((tm, tk), lambda i, j, kk: (i, kk)),
            pl.BlockSpec((None, tk, tn), lambda i, j, kk: (layer, kk, j)),
        ],
        out_specs=pl.BlockSpec((tm, tn), lambda i, j, kk: (i, j)),
        out_shape=jax.ShapeDtypeStruct((m, n), out_dtype),
        scratch_shapes=[pltpu.VMEM((tm, tn), F32)],
        compiler_params=_cparams(("parallel", "parallel", "arbitrary"), 40),
        name="matmul",
    )(a, w)


def _layer_norm(y, g, b):
    mu = jnp.mean(y, axis=-1, keepdims=True)
    yc = y - mu
    var = jnp.mean(yc * yc, axis=-1, keepdims=True)
    return yc * lax.rsqrt(var + LN_EPS) * g + b


def _round_bf16_bits(v):
    return lax.bitcast_convert_type(v.astype(BF16).astype(F32), U32)


def _pack_rows(yn, o_ref, rows):
    half = yn.shape[1] // 2
    packed = (_round_bf16_bits(yn[:, :half]) >> 16) | (_round_bf16_bits(yn[:, half:]) & jnp.uint32(0xFFFF0000))
    for s in range(ROW_SLAB):
        o_ref[pl.ds(s, rows, stride=ROW_SLAB), :] = packed[:, s * LANES:(s + 1) * LANES]


def _unpack_rows(ref, base, rows, dtype):
    lo, hi = [], []
    for s in range(ROW_SLAB):
        u = ref[pl.ds(base + s, rows, stride=ROW_SLAB), :]
        lo.append(lax.bitcast_convert_type(u << 16, F32).astype(dtype))
        hi.append(lax.bitcast_convert_type(u & jnp.uint32(0xFFFF0000), F32).astype(dtype))
    return jnp.concatenate(lo, axis=1), jnp.concatenate(hi, axis=1)


def _res_ln_kernel(x_ref, h_ref, g_ref, b_ref, of_ref, o2_ref, *, alpha, packed):
    yn = _layer_norm(alpha * x_ref[...] + h_ref[...].astype(F32), g_ref[...], b_ref[...])
    of_ref[...] = yn
    if packed:
        _pack_rows(yn, o2_ref, yn.shape[0])
    else:
        o2_ref[...] = yn.astype(BF16)


def residual_layer_norm(x, h, g, b, alpha, packed, tm=256):
    t, d = x.shape
    tm = _tile(t, tm)
    assert d == 2 * ROW_SLAB * LANES or not packed
    if packed:
        o2_shape = jax.ShapeDtypeStruct((t * ROW_SLAB, LANES), U32)
        o2_spec = pl.BlockSpec((tm * ROW_SLAB, LANES), lambda i: (i, 0))
    else:
        o2_shape = jax.ShapeDtypeStruct((t, d), BF16)
        o2_spec = pl.BlockSpec((tm, d), lambda i: (i, 0))
    row = pl.BlockSpec((tm, d), lambda i: (i, 0))
    vec = pl.BlockSpec((1, d), lambda i: (0, 0))
    return pl.pallas_call(
        functools.partial(_res_ln_kernel, alpha=alpha, packed=packed),
        grid=(t // tm,),
        in_specs=[row, row, vec, vec],
        out_specs=[row, o2_spec],
        out_shape=[jax.ShapeDtypeStruct((t, d), F32), o2_shape],
        compiler_params=_cparams(("parallel",), 48),
        name="residual_layer_norm",
    )(x, h, g.reshape(1, d), b.reshape(1, d))


def _gla_kernel(q_ref, k_ref, v_ref, g_ref, gd_ref, wu_ref, bg_ref, ng_ref, o_ref, st_ref, *, n_chunks, q_scale):
    @pl.when(pl.program_id(2) == 0)
    def _():
        st_ref[...] = jnp.zeros_like(st_ref)

    ri = lax.broadcasted_iota(I32, (CHUNK, CHUNK), 0)
    ci = lax.broadcasted_iota(I32, (CHUNK, CHUNK), 1)
    causal = ri >= ci
    tri = causal.astype(BF16)

    def chunk(c, carry):
        rows = pl.ds(pl.multiple_of(c * CHUNK, CHUNK), CHUNK)
        q = q_ref[rows, :].astype(F32) * q_scale
        k = k_ref[rows, :].astype(F32)
        v = v_ref[rows, :]
        gate = jnp.dot(gd_ref[rows, :].astype(BF16), wu_ref[...], preferred_element_type=F32) + bg_ref[...]
        la = (jnp.minimum(gate, 0.0) - jnp.log(1.0 + jnp.exp(-jnp.abs(gate)))) * (1.0 / GLA_GATE_TAU)
        hi = la.astype(BF16)
        r1 = la - hi.astype(F32)
        mid = r1.astype(BF16)
        lo = (r1 - mid.astype(F32)).astype(BF16)
        bc = (jnp.dot(tri, hi, preferred_element_type=F32) + jnp.dot(tri, mid, preferred_element_type=F32)
              + jnp.dot(tri, lo, preferred_element_type=F32))
        bl = bc[CHUNK - 1:CHUNK, :]
        q_e = (q * jnp.exp(bc)).astype(BF16)
        k_e = (k * jnp.exp(-bc)).astype(BF16)
        k_d = (k * jnp.exp(bl - bc)).astype(BF16)
        att = lax.dot_general(q_e, k_e, _TRANS_B, preferred_element_type=F32)
        att = jnp.where(causal, att, 0.0).astype(BF16)
        st = st_ref[...]
        o = jnp.dot(att, v, preferred_element_type=F32) + lax.dot_general(
            q_e, st.astype(BF16), _TRANS_B, preferred_element_type=F32)
        st_ref[...] = st * jnp.exp(bl) + lax.dot_general(v, k_d, _TRANS_A, preferred_element_type=F32)
        on = o * lax.rsqrt(jnp.mean(o * o, axis=-1, keepdims=True) + LN_EPS) * ng_ref[...]
        g = g_ref[rows, :].astype(F32)
        o_ref[rows, :] = (on * (g / (1.0 + jnp.exp(-g)))).astype(o_ref.dtype)
        return carry

    lax.fori_loop(0, n_chunks, chunk, 0, unroll=2)


def gla_core(z, gd, wu, bg, ng, layer, rows_per_step=512):
    b, s, zc = z.shape
    hk = wu.shape[3]
    hv = ng.shape[3]
    assert zc == GLA_HEADS * (2 * hk + 2 * hv) and hv == 2 * hk
    rb = _tile(s, rows_per_step)
    assert rb % (2 * CHUNK) == 0
    kq = GLA_HEADS
    kv = 2 * GLA_HEADS * hk // hv
    kg = kv + GLA_HEADS
    return pl.pallas_call(
        functools.partial(_gla_kernel, n_chunks=rb // CHUNK, q_scale=float(hk) ** -0.5),
        grid=(b, GLA_HEADS, s // rb),
        in_specs=[
            pl.BlockSpec((None, rb, hk), lambda bi, h, r: (bi, r, h)),
            pl.BlockSpec((None, rb, hk), lambda bi, h, r: (bi, r, kq + h)),
            pl.BlockSpec((None, rb, hv), lambda bi, h, r: (bi, r, kv + h)),
            pl.BlockSpec((None, rb, hv), lambda bi, h, r: (bi, r, kg + h)),
            pl.BlockSpec((None, rb, LANES), lambda bi, h, r: (bi, r, 0)),
            pl.BlockSpec((None, None, LANES, hk), lambda bi, h, r: (layer, h, 0, 0)),
            pl.BlockSpec((None, None, 1, hk), lambda bi, h, r: (layer, h, 0, 0)),
            pl.BlockSpec((None, None, 1, hv), lambda bi, h, r: (layer, h, 0, 0)),
        ],
        out_specs=pl.BlockSpec((None, rb, hv), lambda bi, h, r: (bi, r, h)),
        out_shape=jax.ShapeDtypeStruct((b, s, GLA_HEADS * hv), BF16),
        scratch_shapes=[pltpu.VMEM((hv, hk), F32)],
        compiler_params=_cparams(("parallel", "parallel", "arbitrary"), 40),
        name="gla_core",
    )(z, z, z, z, gd, wu, bg, ng)


def _band_attn_kernel(q_ref, k_ref, v_ref, bias_ref, o_ref, kp_ref, vp_ref, *, n_blocks, scale):
    pad = ATT_LEFT_CHUNKS * CHUNK
    s = n_blocks * ATT_QB
    kp_ref[pl.ds(0, pad), :] = jnp.zeros((pad, ATT_HEAD_DIM), kp_ref.dtype)
    vp_ref[pl.ds(0, pad), :] = jnp.zeros((pad, ATT_HEAD_DIM), vp_ref.dtype)
    kp_ref[pl.ds(pad, s), :] = k_ref[...]
    vp_ref[pl.ds(pad, s), :] = v_ref[...]
    last_tile = bias_ref.shape[0] - 1

    def block(i, carry):
        start = pl.multiple_of(i * ATT_QB, ATT_QB)
        q = (q_ref[pl.ds(start, ATT_QB), :].astype(F32) * scale).astype(BF16)
        kb = kp_ref[pl.ds(start, ATT_WIN), :]
        vb = vp_ref[pl.ds(start, ATT_WIN), :]
        sc = lax.dot_general(q, kb, _TRANS_B, preferred_element_type=F32) + bias_ref[jnp.minimum(i, last_tile)]
        p = jnp.exp(sc - jnp.max(sc, axis=-1, keepdims=True))
        den = jnp.sum(p, axis=-1, keepdims=True)
        o = jnp.dot(p.astype(BF16), vb, preferred_element_type=F32) / den
        o_ref[pl.ds(start, ATT_QB), :] = o.astype(o_ref.dtype)
        return carry

    lax.fori_loop(0, n_blocks, block, 0, unroll=4)


def band_attention_core(qkv, bias):
    b, s, c3 = qkv.shape
    h, n_tiles = bias.shape[:2]
    assert c3 == 3 * h * ATT_HEAD_DIM and s % (4 * ATT_QB) == 0
    blk = lambda off: pl.BlockSpec((None, s, ATT_HEAD_DIM), lambda bi, hi: (bi, 0, off + hi))
    return pl.pallas_call(
        functools.partial(_band_attn_kernel, n_blocks=s // ATT_QB, scale=float(ATT_HEAD_DIM) ** -0.5),
        grid=(b, h),
        in_specs=[blk(0), blk(h), blk(2 * h),
                  pl.BlockSpec((None, n_tiles, ATT_QB, ATT_WIN), lambda bi, hi: (hi, 0, 0, 0))],
        out_specs=blk(0),
        out_shape=jax.ShapeDtypeStruct((b, s, h * ATT_HEAD_DIM), BF16),
        scratch_shapes=[pltpu.VMEM((s + ATT_LEFT_CHUNKS * CHUNK, ATT_HEAD_DIM), BF16)] * 2,
        compiler_params=_cparams(("parallel", "parallel"), 40),
        name="band_attention_core",
    )(qkv, qkv, qkv, bias)


def _rel_bias(rel_table):
    h = rel_table.shape[0]
    pad = ATT_LEFT_CHUNKS * CHUNK
    assert ATT_REL_CLIP >= CHUNK and 2 * ATT_REL_CLIP + 1 >= ATT_BAND - ATT_REL_CLIP
    rev = rel_table[:, ::-1].astype(F32)
    lead = pad - ATT_REL_CLIP + CHUNK - 1
    ext = jnp.concatenate([jnp.broadcast_to(rev[:, :1], (h, lead)), rev[:, :ATT_BAND - ATT_REL_CLIP]], axis=1)
    band = jnp.stack([ext[:, CHUNK - 1 - i:CHUNK - 1 - i + ATT_BAND] for i in range(CHUNK)], axis=1)
    neg = jnp.full((h, CHUNK, ATT_WIN), -jnp.inf, F32)
    window = jnp.concatenate(
        [lax.dynamic_update_slice(neg, band, (0, 0, c * CHUNK)) for c in range(ATT_QUERY_CHUNKS)], axis=1)
    n_tiles = -(-pad // ATT_QB) + 1
    slot = np.arange(ATT_WIN)[None, :]
    keep = np.stack([np.broadcast_to(slot >= pad - n * ATT_QB, (ATT_QB, ATT_WIN)) for n in range(n_tiles)])
    return jnp.where(keep[None], window[:, None], -jnp.inf)


def _mem_attn_kernel(q_ref, kv_ref, o_ref, *, head_dim, scale):
    width = MEM_HEADS * head_dim
    for h in range(MEM_HEADS):
        cols = slice(h * head_dim, (h + 1) * head_dim)
        sc = lax.dot_general(q_ref[:, cols], kv_ref[:, cols], _TRANS_B, preferred_element_type=F32) * scale
        p = jnp.exp(sc - jnp.max(sc, axis=-1, keepdims=True))
        den = jnp.sum(p, axis=-1, keepdims=True)
        vh = kv_ref[:, width + h * head_dim:width + (h + 1) * head_dim]
        o_ref[:, cols] = (jnp.dot(p.astype(BF16), vh, preferred_element_type=F32) / den).astype(o_ref.dtype)


def memory_attention_core(q, kv, tq=512):
    b, s, width = q.shape
    m = kv.shape[1]
    head_dim = width // MEM_HEADS
    tq = _tile(s, tq)
    return pl.pallas_call(
        functools.partial(_mem_attn_kernel, head_dim=head_dim, scale=float(head_dim) ** -0.5),
        grid=(b, s // tq),
        in_specs=[
            pl.BlockSpec((None, tq, width), lambda bi, r: (bi, r, 0)),
            pl.BlockSpec((None, m, 2 * width), lambda bi, r: (bi, 0, 0)),
        ],
        out_specs=pl.BlockSpec((None, tq, width), lambda bi, r: (bi, r, 0)),
        out_shape=jax.ShapeDtypeStruct((b, s, width), BF16),
        compiler_params=_cparams(("parallel", "parallel"), 32),
        name="memory_attention_core",
    )(q, kv)


def _top2_of4(a, b, c, d):
    hi1, lo1 = jnp.maximum(a, b), jnp.minimum(a, b)
    hi2, lo2 = jnp.maximum(c, d), jnp.minimum(c, d)
    return jnp.maximum(hi1, hi2) + jnp.maximum(jnp.minimum(hi1, hi2), jnp.maximum(lo1, lo2))


def _argmax_first(vals):
    best, idx = vals[0], jnp.zeros(vals[0].shape, I32)
    for j in range(1, len(vals)):
        better = vals[j] > best
        best = jnp.where(better, vals[j], best)
        idx = jnp.where(better, j, idx)
    return best, idx


def _router_kernel(x_ref, wr_ref, br_ref, route_ref, wts_ref, cnt_ref, upper_ref, *, n_experts):
    tr = x_ref.shape[0]
    per_group = n_experts // MOE_GROUPS

    @pl.when(pl.program_id(0) == 0)
    def _():
        cnt_ref[...] = jnp.zeros_like(cnt_ref)
        earlier = lax.broadcasted_iota(I32, (tr, tr), 0) < lax.broadcasted_iota(I32, (tr, tr), 1)
        upper_ref[...] = earlier.astype(BF16)

    logits = jnp.dot(x_ref[...], wr_ref[...], preferred_element_type=F32, precision=lax.Precision.HIGHEST)
    lt = logits.T[:n_experts, :] + br_ref[:n_experts, :]
    ex = jnp.exp(lt - jnp.max(lt, axis=0, keepdims=True))
    probs = ex / jnp.sum(ex, axis=0, keepdims=True)
    p = [probs[e:e + 1, :] for e in range(n_experts)]

    assert per_group == 4
    scores = [_top2_of4(*p[per_group * g:per_group * (g + 1)]) for g in range(MOE_GROUPS)]
    _, best_group = _argmax_first(scores)
    in_group = [sum(jnp.where(best_group == g, p[per_group * g + j], 0.0) for g in range(MOE_GROUPS))
                for j in range(per_group)]
    p1, j1 = _argmax_first(in_group)
    p2, j2 = _argmax_first([jnp.where(j1 == j, -1.0, in_group[j]) for j in range(per_group)])
    e1 = best_group * per_group + j1
    e2 = best_group * per_group + j2
    w1 = p1 / (p1 + p2)
    w2 = p2 / (p1 + p2)

    erow = lax.broadcasted_iota(I32, (n_experts, tr), 0)
    hit1 = erow == e1
    hit2 = erow == e2
    onehot = hit1.astype(F32) + hit2.astype(F32)
    before = cnt_ref[:, 0:1] + jnp.dot(onehot.astype(BF16), upper_ref[...], preferred_element_type=F32)
    rank1 = jnp.sum(jnp.where(hit1, before, 0.0), axis=0, keepdims=True).astype(I32)
    rank2 = jnp.sum(jnp.where(hit2, before, 0.0), axis=0, keepdims=True).astype(I32)
    cnt_ref[...] += jnp.sum(onehot, axis=1, keepdims=True)

    rrow = lax.broadcasted_iota(I32, (SUBLANES, tr), 0)
    route_ref[...] = jnp.where(rrow == 0, e1, jnp.where(rrow == 1, e2, jnp.where(rrow == 2, rank1, rank2)))
    wrow = lax.broadcasted_iota(I32, (LANES, tr), 0)
    wts_ref[...] = jnp.where(wrow == 0, w1, jnp.where(wrow == 1, w2, 0.0)).T


def moe_router(x, wr, br, n_experts, tr=1024):
    t, d = x.shape
    tr = _tile(t, tr)
    return pl.pallas_call(
        functools.partial(_router_kernel, n_experts=n_experts),
        grid=(t // tr,),
        in_specs=[
            pl.BlockSpec((tr, d), lambda i: (i, 0)),
            pl.BlockSpec((d, LANES), lambda i: (0, 0)),
            pl.BlockSpec((LANES, 1), lambda i: (0, 0)),
        ],
        out_specs=[
            pl.BlockSpec((SUBLANES, tr), lambda i: (0, i)),
            pl.BlockSpec((tr, LANES), lambda i: (i, 0)),
            pl.BlockSpec((n_experts, LANES), lambda i: (0, 0)),
        ],
        out_shape=[
            jax.ShapeDtypeStruct((SUBLANES, t), I32),
            jax.ShapeDtypeStruct((t, LANES), F32),
            jax.ShapeDtypeStruct((n_experts, LANES), F32),
        ],
        scratch_shapes=[pltpu.VMEM((tr, tr), BF16)],
        compiler_params=_cparams(("arbitrary",), 56),
        name="moe_router",
    )(x, wr, br)


def _row_copy(src_ref, src_row, dst_ref, dst_row, sem):
    return pltpu.make_async_copy(
        src_ref.at[pl.ds(pl.multiple_of(src_row * ROW_SLAB, ROW_SLAB), ROW_SLAB), :],
        dst_ref.at[pl.ds(pl.multiple_of(dst_row * ROW_SLAB, ROW_SLAB), ROW_SLAB), :],
        sem)


def _rows_wait(src_ref, dst_ref, n_rows, sem):
    span = pl.ds(0, n_rows * ROW_SLAB)
    pltpu.make_async_copy(src_ref.at[span, :], dst_ref.at[span, :], sem).wait()


def _dispatch_kernel(pos_ref, xg_ref, init_ref, xs_ref, sem, *, td):
    del init_ref

    def issue(t, carry):
        for slot in range(2):
            _row_copy(xg_ref, t, xs_ref, pos_ref[0, 0, slot * td + t], sem).start()
        return carry

    lax.fori_loop(0, td, issue, 0, unroll=4)
    for _ in range(2):
        _rows_wait(xg_ref, xs_ref, td, sem)


def moe_dispatch(xg, pos_blocks, n_sorted_rows, td):
    t = xg.shape[0] // ROW_SLAB
    init = jnp.zeros((n_sorted_rows * ROW_SLAB, LANES), U32)
    return pl.pallas_call(
        functools.partial(_dispatch_kernel, td=td),
        grid=(t // td,),
        in_specs=[
            pl.BlockSpec((1, 1, 2 * td), lambda i: (i, 0, 0), memory_space=pltpu.SMEM),
            pl.BlockSpec((td * ROW_SLAB, LANES), lambda i: (i, 0)),
            pl.BlockSpec(memory_space=pl.ANY),
        ],
        out_specs=pl.BlockSpec(memory_space=pl.ANY),
        out_shape=jax.ShapeDtypeStruct(init.shape, U32),
        scratch_shapes=[pltpu.SemaphoreType.DMA],
        input_output_aliases={2: 0},
        compiler_params=_cparams(("arbitrary",), 32),
        name="moe_dispatch",
    )(pos_blocks, xg, init)


def _expert_kernel(te_ref, nu_ref, xs_ref, wgu_ref, wd_ref, y_ref, *, tm):
    del te_ref
    i = pl.program_id(0)

    @pl.when(i < nu_ref[0])
    def _():
        half = wgu_ref.shape[0] // 2
        ff = wd_ref.shape[0]
        x_lo, x_hi = _unpack_rows(xs_ref, 0, tm, BF16)
        gu = (jnp.dot(x_lo, wgu_ref[:half, :], preferred_element_type=F32)
              + jnp.dot(x_hi, wgu_ref[half:, :], preferred_element_type=F32))
        gate, up = gu[:, :ff], gu[:, ff:]
        hid = (gate / (1.0 + jnp.exp(-gate)) * up).astype(BF16)
        _pack_rows(jnp.dot(hid, wd_ref[...], preferred_element_type=F32), y_ref, tm)

    @pl.when(i >= nu_ref[0])
    def _():
        y_ref[...] = jnp.zeros_like(y_ref)


def moe_experts(xs, wgu, wd, layer, tile_expert, n_used, tm):
    n_tiles = xs.shape[0] // (tm * ROW_SLAB)
    _, _, d, ff2 = wgu.shape
    rows = pl.BlockSpec((tm * ROW_SLAB, LANES), lambda i, te, nu: (i, 0))
    return pl.pallas_call(
        functools.partial(_expert_kernel, tm=tm),
        grid_spec=pltpu.PrefetchScalarGridSpec(
            num_scalar_prefetch=2,
            grid=(n_tiles,),
            in_specs=[
                rows,
                pl.BlockSpec((None, None, d, ff2), lambda i, te, nu: (layer, te[i], 0, 0)),
                pl.BlockSpec((None, None, ff2 // 2, d), lambda i, te, nu: (layer, te[i], 0, 0)),
            ],
            out_specs=rows,
        ),
        out_shape=jax.ShapeDtypeStruct(xs.shape, U32),
        compiler_params=_cparams(("arbitrary",), 56),
        name="moe_experts",
    )(tile_expert, n_used, xs, wgu, wd)


def _combine_ln_kernel(pos_ref, wts_ref, x_ref, ys_ref, g_ref, b_ref, of_ref, ob_ref, buf_ref, sem, *, tc, alpha):
    def issue(t, carry):
        for slot in range(2):
            _row_copy(ys_ref, pos_ref[0, 0, slot * tc + t], buf_ref, slot * tc + t, sem).start()
        return carry

    lax.fori_loop(0, tc, issue, 0, unroll=4)
    _rows_wait(ys_ref, buf_ref, 2 * tc, sem)
    y = alpha * x_ref[...]
    for slot in range(2):
        lo, hi = _unpack_rows(buf_ref, slot * tc * ROW_SLAB, tc, F32)
        y = y + wts_ref[:, slot:slot + 1] * jnp.concatenate([lo, hi], axis=1)
    yn = _layer_norm(y, g_ref[...], b_ref[...])
    of_ref[...] = yn
    ob_ref[...] = yn.astype(BF16)


def moe_combine_layer_norm(x, ys, pos_blocks, wts, g, b, alpha, tc):
    t, d = x.shape
    row = pl.BlockSpec((tc, d), lambda i: (i, 0))
    vec = pl.BlockSpec((1, d), lambda i: (0, 0))
    return pl.pallas_call(
        functools.partial(_combine_ln_kernel, tc=tc, alpha=alpha),
        grid=(t // tc,),
        in_specs=[
            pl.BlockSpec((1, 1, 2 * tc), lambda i: (i, 0, 0), memory_space=pltpu.SMEM),
            pl.BlockSpec((tc, LANES), lambda i: (i, 0)),
            row,
            pl.BlockSpec(memory_space=pl.ANY),
            vec,
            vec,
        ],
        out_specs=[row, row],
        out_shape=[jax.ShapeDtypeStruct((t, d), F32), jax.ShapeDtypeStruct((t, d), BF16)],
        scratch_shapes=[pltpu.VMEM((2 * tc * ROW_SLAB, LANES), U32), pltpu.SemaphoreType.DMA],
        compiler_params=_cparams(("arbitrary",), 48),
        name="moe_combine_layer_norm",
    )(pos_blocks, wts, x, ys, g.reshape(1, d), b.reshape(1, d))


def _pos_blocks(pos, tile):
    t = pos.shape[1]
    return pos.reshape(2, t // tile, tile).transpose(1, 0, 2).reshape(t // tile, 1, 2 * tile)


def grouped_moe_layer_norm(xf, xg, wr, br, wgu, wd, layer, g, b, alpha, tm=256, td=512, tc=256):
    t, d = xf.shape
    n_experts = wgu.shape[1]
    route, wts, counts = moe_router(xf, wr, br, n_experts)
    cnt = counts[:, 0].astype(I32)
    padded = (cnt + tm - 1) // tm * tm
    ends = jnp.cumsum(padded)
    offs = ends - padded
    expert_ids = jnp.arange(n_experts, dtype=I32)
    pos = jnp.sum(jnp.where(route[0:2, :, None] == expert_ids, offs, 0), axis=-1) + route[2:4]
    n_tiles = (2 * t) // tm + n_experts
    tile_starts = jnp.arange(n_tiles, dtype=I32) * tm
    tile_expert = jnp.minimum(jnp.sum((ends[None, :] <= tile_starts[:, None]).astype(I32), axis=1), n_experts - 1)
    n_used = (ends[-1:] // tm).astype(I32)
    xs = moe_dispatch(xg, _pos_blocks(pos, td), n_tiles * tm, td)
    ys = moe_experts(xs, wgu, wd, layer, tile_expert, n_used, tm)
    return moe_combine_layer_norm(xf, ys, _pos_blocks(pos, tc), wts, g, b, alpha, tc)


def kernel(x, mem, gla_w_in, gla_w_gate_down, gla_w_gate_up, gla_b_gate, gla_norm_g, gla_w_o, att_w_qkv, att_rel_bias, att_w_o, mx_w_q, mx_w_kv, mx_w_o, router_w, router_b, moe_w_gu, moe_w_down, ln_g, ln_b):
    bsz, seq, d = x.shape
    t = bsz * seq
    depth = ln_g.shape[0]
    alpha = (2.0 * depth) ** 0.25
    n_experts = router_w.shape[1]
    n_gla = gla_w_in.shape[0]
    hk = gla_w_gate_up.shape[2] // GLA_HEADS
    hv = gla_norm_g.shape[1] // GLA_HEADS
    rank = gla_w_gate_down.shape[2]
    mem_len = mem.shape[1]

    xf = x.reshape(t, d)
    xb = xf.astype(BF16)
    mem_b = mem.reshape(bsz * mem_len, d).astype(BF16)
    wr = jnp.pad(router_w, ((0, 0), (0, LANES - n_experts)))
    br = jnp.pad(router_b, (0, LANES - n_experts)).reshape(LANES, 1)
    w_in, w_go, w_qkv, w_ao = (w.astype(BF16) for w in (gla_w_in, gla_w_o, att_w_qkv, att_w_o))
    w_mq, w_mkv, w_mo = (w.astype(BF16) for w in (mx_w_q, mx_w_kv, mx_w_o))
    w_gu, w_dn = moe_w_gu.astype(BF16), moe_w_down.astype(BF16)
    w_gd = jnp.pad(gla_w_gate_down, ((0, 0), (0, 0), (0, LANES - rank))).astype(BF16)
    w_gup = jnp.pad(gla_w_gate_up.reshape(n_gla, rank, GLA_HEADS, hk).transpose(0, 2, 1, 3),
                    ((0, 0), (0, 0), (0, LANES - rank), (0, 0))).astype(BF16)
    b_gate = gla_b_gate.reshape(n_gla, GLA_HEADS, 1, hk)
    n_gain = gla_norm_g.reshape(n_gla, GLA_HEADS, 1, hv)

    for i in range(depth):
        j = i // 2
        if i % 2 == 0:
            z = matmul(xb, w_in, j, BF16)
            gd = matmul(xb, w_gd, j, F32)
            o = gla_core(z.reshape(bsz, seq, -1), gd.reshape(bsz, seq, LANES), w_gup, b_gate, n_gain, j)
            h = matmul(o.reshape(t, -1), w_go, j, BF16)
        else:
            qkv = matmul(xb, w_qkv, j, BF16)
            o = band_attention_core(qkv.reshape(bsz, seq, -1), _rel_bias(att_rel_bias[j]))
            h = matmul(o.reshape(t, -1), w_ao, j, BF16)
        xf, xb = residual_layer_norm(xf, h, ln_g[i, 0], ln_b[i, 0], alpha, packed=False)

        q = matmul(xb, w_mq, i, BF16)
        kv = matmul(mem_b, w_mkv, i, BF16)
        o = memory_attention_core(q.reshape(bsz, seq, -1), kv.reshape(bsz, mem_len, -1))
        h = matmul(o.reshape(t, -1), w_mo, i, BF16)
        xf, xg = residual_layer_norm(xf, h, ln_g[i, 1], ln_b[i, 1], alpha, packed=True)

        xf, xb = grouped_moe_layer_norm(xf, xg, wr, br, w_gu, w_dn, i, ln_g[i, 2], ln_b[i, 2], alpha)
    return xf.reshape(bsz, seq, d)
```

```python
import functools

import jax
import jax.numpy as jnp
import numpy as np
from jax import lax
from jax.experimental import pallas as pl
from jax.experimental.pallas import tpu as pltpu

F32 = jnp.float32
BF16 = jnp.bfloat16
U32 = jnp.uint32
I32 = jnp.int32

CHUNK = 64
LN_EPS = 1e-5
GLA_HEADS = 4
GLA_GATE_TAU = 16.0
ATT_HEAD_DIM = 128
ATT_LEFT_CHUNKS = 8
ATT_BAND = (ATT_LEFT_CHUNKS + 1) * CHUNK
ATT_REL_CLIP = 256
ATT_QUERY_CHUNKS = 4
ATT_QB = ATT_QUERY_CHUNKS * CHUNK
ATT_WIN = (ATT_LEFT_CHUNKS + ATT_QUERY_CHUNKS) * CHUNK
MEM_HEADS = 4
MOE_GROUPS = 4

LANES = 128
SUBLANES = 8

ROW_SLAB = 16
COMBINE_ROWS = 16

_TRANS_B = (((1,), (1,)), ((), ()))
_TRANS_A = (((0,), (0,)), ((), ()))


def _cparams(sem, vmem_mb):
    return pltpu.CompilerParams(dimension_semantics=sem, vmem_limit_bytes=vmem_mb * 1024 * 1024)


def _tile(dim, pref):
    t = min(dim, pref)
    assert dim % t == 0, (dim, pref)
    return t


def _matmul_kernel(a_ref, w_ref, o_ref):
    o_ref[...] = jnp.dot(a_ref[...], w_ref[...], preferred_element_type=F32).astype(o_ref.dtype)


def matmul(a, w, layer, out_dtype, tm=1024, tn=1024):
    m, k = a.shape
    _, k2, n = w.shape
    assert k == k2
    tm, tn = _tile(m, tm), _tile(n, tn)
    return pl.pallas_call(
        _matmul_kernel,
        grid=(m // tm, n // tn),
        in_specs=[pl.BlockSpec((tm, k), lambda i, j: (i, 0)), pl.BlockSpec((None, k, tn), lambda i, j: (layer, 0, j))],
        out_specs=pl.BlockSpec((tm, tn), lambda i, j: (i, j)),
        out_shape=jax.ShapeDtypeStruct((m, n), out_dtype),
        compiler_params=_cparams(("parallel", "parallel"), 56),
        name="matmul",
    )(a, w)


def _layer_norm(y, g, b):
    mu = jnp.mean(y, axis=-1, keepdims=True)
    yc = y - mu
    var = jnp.mean(yc * yc, axis=-1, keepdims=True)
    return yc * lax.rsqrt(var + LN_EPS) * g + b


def _round_bf16_bits(v):
    return lax.bitcast_convert_type(v.astype(BF16).astype(F32), U32)


def _pack_rows(yn, o_ref, rows):
    half = yn.shape[1] // 2
    packed = (_round_bf16_bits(yn[:, :half]) >> 16) | (_round_bf16_bits(yn[:, half:]) & jnp.uint32(0xFFFF0000))
    for s in range(ROW_SLAB):
        o_ref[pl.ds(s, rows, stride=ROW_SLAB), :] = packed[:, s * LANES:(s + 1) * LANES]


def _unpack_rows(ref, base, rows, dtype):
    lo, hi = [], []
    for s in range(ROW_SLAB):
        u = ref[pl.ds(base + s, rows, stride=ROW_SLAB), :]
        lo.append(lax.bitcast_convert_type(u << 16, F32).astype(dtype))
        hi.append(lax.bitcast_convert_type(u & jnp.uint32(0xFFFF0000), F32).astype(dtype))
    return jnp.concatenate(lo, axis=1), jnp.concatenate(hi, axis=1)


def _res_ln_proj_kernel(x_ref, h_ref, g_ref, b_ref, wq_ref, of_ref, oq_ref, *, alpha):
    yn = _layer_norm(alpha * x_ref[...] + h_ref[...].astype(F32), g_ref[...], b_ref[...])
    of_ref[...] = yn
    oq_ref[...] = jnp.dot(yn.astype(BF16), wq_ref[...], preferred_element_type=F32).astype(oq_ref.dtype)


def residual_layer_norm_proj(x, h, g, b, wq, layer, alpha, tm=256):
    t, d = x.shape
    nq = wq.shape[2]
    tm = _tile(t, tm)
    row = pl.BlockSpec((tm, d), lambda i: (i, 0))
    vec = pl.BlockSpec((1, d), lambda i: (0, 0))
    return pl.pallas_call(
        functools.partial(_res_ln_proj_kernel, alpha=alpha),
        grid=(t // tm,),
        in_specs=[row, row, vec, vec, pl.BlockSpec((None, d, nq), lambda i: (layer, 0, 0))],
        out_specs=[row, pl.BlockSpec((tm, nq), lambda i: (i, 0))],
        out_shape=[jax.ShapeDtypeStruct((t, d), F32), jax.ShapeDtypeStruct((t, nq), BF16)],
        compiler_params=_cparams(("parallel",), 56),
        name="residual_layer_norm_proj",
    )(x, h, g.reshape(1, d), b.reshape(1, d), wq)


def _memory_attention(q_ref, kv_ref):
    width = q_ref.shape[1]
    head_dim = width // MEM_HEADS
    scale = float(head_dim) ** -0.5
    outs = []
    for h in range(MEM_HEADS):
        cols = slice(h * head_dim, (h + 1) * head_dim)
        sc = lax.dot_general(q_ref[:, cols], kv_ref[:, cols], _TRANS_B, preferred_element_type=F32) * scale
        p = jnp.exp(sc - jnp.max(sc, axis=-1, keepdims=True))
        den = jnp.sum(p, axis=-1, keepdims=True)
        vh = kv_ref[:, width + h * head_dim:width + (h + 1) * head_dim]
        outs.append((jnp.dot(p.astype(BF16), vh, preferred_element_type=F32) / den).astype(BF16))
    return jnp.concatenate(outs, axis=1)


def _mem_attn_proj_kernel(q_ref, kv_ref, wo_ref, h_ref):
    h_ref[...] = jnp.dot(_memory_attention(q_ref, kv_ref), wo_ref[...], preferred_element_type=F32).astype(h_ref.dtype)


def memory_attention_proj(q, kv, wo, layer, tm=512):
    t, width = q.shape
    bsz, m, w2 = kv.shape
    d = wo.shape[2]
    tm = _tile(t // bsz, tm)
    tiles_per_batch = t // bsz // tm
    assert w2 == 2 * width
    return pl.pallas_call(
        _mem_attn_proj_kernel,
        grid=(t // tm,),
        in_specs=[pl.BlockSpec((tm, width), lambda i: (i, 0)),
                  pl.BlockSpec((None, m, w2), lambda i: (i // tiles_per_batch, 0, 0)),
                  pl.BlockSpec((None, width, d), lambda i: (layer, 0, 0))],
        out_specs=pl.BlockSpec((tm, d), lambda i: (i, 0)),
        out_shape=jax.ShapeDtypeStruct((t, d), BF16),
        compiler_params=_cparams(("parallel",), 48),
        name="memory_attention_proj",
    )(q, kv, wo)


def _res_ln_route_kernel(x_ref, h_ref, g_ref, b_ref, wr_ref, of_ref, og_ref, lg_ref, *, alpha):
    yn = _layer_norm(alpha * x_ref[...] + h_ref[...].astype(F32), g_ref[...], b_ref[...])
    of_ref[...] = yn
    _pack_rows(yn, og_ref, yn.shape[0])
    y_hi = yn.astype(BF16)
    y_lo = (yn - y_hi.astype(F32)).astype(BF16)
    both = jnp.dot(y_hi, wr_ref[...], preferred_element_type=F32)
    lg_ref[...] = both[:, :LANES] + both[:, LANES:] + jnp.dot(y_lo, wr_ref[:, :LANES], preferred_element_type=F32)


def residual_layer_norm_route(x, h, g, b, router_w2, alpha, tm=256):
    t, d = x.shape
    tm = _tile(t, tm)
    assert d == 2 * ROW_SLAB * LANES
    row = pl.BlockSpec((tm, d), lambda i: (i, 0))
    vec = pl.BlockSpec((1, d), lambda i: (0, 0))
    return pl.pallas_call(
        functools.partial(_res_ln_route_kernel, alpha=alpha),
        grid=(t // tm,),
        in_specs=[row, row, vec, vec, pl.BlockSpec((d, 2 * LANES), lambda i: (0, 0))],
        out_specs=[row, pl.BlockSpec((tm * ROW_SLAB, LANES), lambda i: (i, 0)), pl.BlockSpec((tm, LANES), lambda i: (i, 0))],
        out_shape=[jax.ShapeDtypeStruct((t, d), F32), jax.ShapeDtypeStruct((t * ROW_SLAB, LANES), U32),
                   jax.ShapeDtypeStruct((t, LANES), F32)],
        compiler_params=_cparams(("parallel",), 48),
        name="residual_layer_norm_route",
    )(x, h, g.reshape(1, d), b.reshape(1, d), router_w2)


def _gla_kernel(q_ref, k_ref, v_ref, g_ref, gd_ref, wu_ref, bg_ref, ng_ref, o_ref, st_ref, *, n_chunks, q_scale):
    @pl.when(pl.program_id(2) == 0)
    def _():
        st_ref[...] = jnp.zeros_like(st_ref)

    ri = lax.broadcasted_iota(I32, (CHUNK, CHUNK), 0)
    ci = lax.broadcasted_iota(I32, (CHUNK, CHUNK), 1)
    causal = ri >= ci
    tri = causal.astype(BF16)

    def chunk(c, carry):
        rows = pl.ds(pl.multiple_of(c * CHUNK, CHUNK), CHUNK)
        q = q_ref[rows, :].astype(F32) * q_scale
        k = k_ref[rows, :].astype(F32)
        v = v_ref[rows, :]
        gate = jnp.dot(gd_ref[rows, :].astype(BF16), wu_ref[...], preferred_element_type=F32) + bg_ref[...]
        la = (jnp.minimum(gate, 0.0) - jnp.log(1.0 + jnp.exp(-jnp.abs(gate)))) * (1.0 / GLA_GATE_TAU)
        hi = la.astype(BF16)
        r1 = la - hi.astype(F32)
        mid = r1.astype(BF16)
        lo = (r1 - mid.astype(F32)).astype(BF16)
        bc = (jnp.dot(tri, hi, preferred_element_type=F32) + jnp.dot(tri, mid, preferred_element_type=F32)
              + jnp.dot(tri, lo, preferred_element_type=F32))
        bl = bc[CHUNK - 1:CHUNK, :]
        q_e = (q * jnp.exp(bc)).astype(BF16)
        k_e = (k * jnp.exp(-bc)).astype(BF16)
        k_d = (k * jnp.exp(bl - bc)).astype(BF16)
        att = lax.dot_general(q_e, k_e, _TRANS_B, preferred_element_type=F32)
        att = jnp.where(causal, att, 0.0).astype(BF16)
        st = st_ref[...]
        o = jnp.dot(att, v, preferred_element_type=F32) + lax.dot_general(
            q_e, st.astype(BF16), _TRANS_B, preferred_element_type=F32)
        st_ref[...] = st * jnp.exp(bl) + lax.dot_general(v, k_d, _TRANS_A, preferred_element_type=F32)
        on = o * lax.rsqrt(jnp.mean(o * o, axis=-1, keepdims=True) + LN_EPS) * ng_ref[...]
        g = g_ref[rows, :].astype(F32)
        o_ref[rows, :] = (on * (g / (1.0 + jnp.exp(-g)))).astype(o_ref.dtype)
        return carry

    lax.fori_loop(0, n_chunks, chunk, 0, unroll=2)


def gla_core(z, gd, wu, bg, ng, layer, rows_per_step=512):
    b, s, zc = z.shape
    hk = wu.shape[3]
    hv = ng.shape[3]
    assert zc == GLA_HEADS * (2 * hk + 2 * hv) and hv == 2 * hk
    rb = _tile(s, rows_per_step)
    assert rb % (2 * CHUNK) == 0
    kq = GLA_HEADS
    kv = 2 * GLA_HEADS * hk // hv
    kg = kv + GLA_HEADS
    return pl.pallas_call(
        functools.partial(_gla_kernel, n_chunks=rb // CHUNK, q_scale=float(hk) ** -0.5),
        grid=(b, GLA_HEADS, s // rb),
        in_specs=[
            pl.BlockSpec((None, rb, hk), lambda bi, h, r: (bi, r, h)),
            pl.BlockSpec((None, rb, hk), lambda bi, h, r: (bi, r, kq + h)),
            pl.BlockSpec((None, rb, hv), lambda bi, h, r: (bi, r, kv + h)),
            pl.BlockSpec((None, rb, hv), lambda bi, h, r: (bi, r, kg + h)),
            pl.BlockSpec((None, rb, LANES), lambda bi, h, r: (bi, r, 0)),
            pl.BlockSpec((None, None, LANES, hk), lambda bi, h, r: (layer, h, 0, 0)),
            pl.BlockSpec((None, None, 1, hk), lambda bi, h, r: (layer, h, 0, 0)),
            pl.BlockSpec((None, None, 1, hv), lambda bi, h, r: (layer, h, 0, 0)),
        ],
        out_specs=pl.BlockSpec((None, rb, hv), lambda bi, h, r: (bi, r, h)),
        out_shape=jax.ShapeDtypeStruct((b, s, GLA_HEADS * hv), BF16),
        scratch_shapes=[pltpu.VMEM((hv, hk), F32)],
        compiler_params=_cparams(("parallel", "parallel", "arbitrary"), 40),
        name="gla_core",
    )(z, z, z, z, gd, wu, bg, ng)


def _band_attn_kernel(q_ref, k_ref, v_ref, bias_ref, o_ref, kp_ref, vp_ref, *, n_blocks, scale):
    pad = ATT_LEFT_CHUNKS * CHUNK
    s = n_blocks * ATT_QB
    kp_ref[pl.ds(0, pad), :] = jnp.zeros((pad, ATT_HEAD_DIM), kp_ref.dtype)
    vp_ref[pl.ds(0, pad), :] = jnp.zeros((pad, ATT_HEAD_DIM), vp_ref.dtype)
    kp_ref[pl.ds(pad, s), :] = k_ref[...]
    vp_ref[pl.ds(pad, s), :] = v_ref[...]
    last_tile = bias_ref.shape[0] - 1

    def block(i, carry):
        start = pl.multiple_of(i * ATT_QB, ATT_QB)
        q = (q_ref[pl.ds(start, ATT_QB), :].astype(F32) * scale).astype(BF16)
        kb = kp_ref[pl.ds(start, ATT_WIN), :]
        vb = vp_ref[pl.ds(start, ATT_WIN), :]
        sc = lax.dot_general(q, kb, _TRANS_B, preferred_element_type=F32) + bias_ref[jnp.minimum(i, last_tile)]
        p = jnp.exp(sc - jnp.max(sc, axis=-1, keepdims=True))
        den = jnp.sum(p, axis=-1, keepdims=True)
        o = jnp.dot(p.astype(BF16), vb, preferred_element_type=F32) / den
        o_ref[pl.ds(start, ATT_QB), :] = o.astype(o_ref.dtype)
        return carry

    lax.fori_loop(0, n_blocks, block, 0, unroll=4)


def band_attention_core(qkv, bias):
    b, s, c3 = qkv.shape
    h, n_tiles = bias.shape[:2]
    assert c3 == 3 * h * ATT_HEAD_DIM and s % (4 * ATT_QB) == 0
    blk = lambda off: pl.BlockSpec((None, s, ATT_HEAD_DIM), lambda bi, hi: (bi, 0, off + hi))
    return pl.pallas_call(
        functools.partial(_band_attn_kernel, n_blocks=s // ATT_QB, scale=float(ATT_HEAD_DIM) ** -0.5),
        grid=(b, h),
        in_specs=[blk(0), blk(h), blk(2 * h),
                  pl.BlockSpec((None, n_tiles, ATT_QB, ATT_WIN), lambda bi, hi: (hi, 0, 0, 0))],
        out_specs=blk(0),
        out_shape=jax.ShapeDtypeStruct((b, s, h * ATT_HEAD_DIM), BF16),
        scratch_shapes=[pltpu.VMEM((s + ATT_LEFT_CHUNKS * CHUNK, ATT_HEAD_DIM), BF16)] * 2,
        compiler_params=_cparams(("parallel", "parallel"), 40),
        name="band_attention_core",
    )(qkv, qkv, qkv, bias)


def _rel_bias(rel_table):
    h = rel_table.shape[0]
    pad = ATT_LEFT_CHUNKS * CHUNK
    assert ATT_REL_CLIP >= CHUNK and 2 * ATT_REL_CLIP + 1 >= ATT_BAND - ATT_REL_CLIP
    rev = rel_table[:, ::-1].astype(F32)
    lead = pad - ATT_REL_CLIP + CHUNK - 1
    ext = jnp.concatenate([jnp.broadcast_to(rev[:, :1], (h, lead)), rev[:, :ATT_BAND - ATT_REL_CLIP]], axis=1)
    band = jnp.stack([ext[:, CHUNK - 1 - i:CHUNK - 1 - i + ATT_BAND] for i in range(CHUNK)], axis=1)
    neg = jnp.full((h, CHUNK, ATT_WIN), -jnp.inf, F32)
    window = jnp.concatenate(
        [lax.dynamic_update_slice(neg, band, (0, 0, c * CHUNK)) for c in range(ATT_QUERY_CHUNKS)], axis=1)
    n_tiles = -(-pad // ATT_QB) + 1
    slot = np.arange(ATT_WIN)[None, :]
    keep = np.stack([np.broadcast_to(slot >= pad - n * ATT_QB, (ATT_QB, ATT_WIN)) for n in range(n_tiles)])
    return jnp.where(keep[None], window[:, None], -jnp.inf)


def _top2_of4(a, b, c, d):
    hi1, lo1 = jnp.maximum(a, b), jnp.minimum(a, b)
    hi2, lo2 = jnp.maximum(c, d), jnp.minimum(c, d)
    return jnp.maximum(hi1, hi2) + jnp.maximum(jnp.minimum(hi1, hi2), jnp.maximum(lo1, lo2))


def _argmax_first(vals):
    best, idx = vals[0], jnp.zeros(vals[0].shape, I32)
    for j in range(1, len(vals)):
        better = vals[j] > best
        best = jnp.where(better, vals[j], best)
        idx = jnp.where(better, j, idx)
    return best, idx


def _router_kernel(lg_ref, br_ref, route_ref, wts_ref, cnt_ref, upper_ref, *, n_experts):
    tr = lg_ref.shape[0]
    per_group = n_experts // MOE_GROUPS

    @pl.when(pl.program_id(0) == 0)
    def _():
        cnt_ref[...] = jnp.zeros_like(cnt_ref)
        earlier = lax.broadcasted_iota(I32, (tr, tr), 0) < lax.broadcasted_iota(I32, (tr, tr), 1)
        upper_ref[...] = earlier.astype(BF16)

    lt = lg_ref[...].T[:n_experts, :] + br_ref[:n_experts, :]
    ex = jnp.exp(lt - jnp.max(lt, axis=0, keepdims=True))
    probs = ex / jnp.sum(ex, axis=0, keepdims=True)
    p = [probs[e:e + 1, :] for e in range(n_experts)]

    assert per_group == 4
    scores = [_top2_of4(*p[per_group * g:per_group * (g + 1)]) for g in range(MOE_GROUPS)]
    _, best_group = _argmax_first(scores)
    in_group = [sum(jnp.where(best_group == g, p[per_group * g + j], 0.0) for g in range(MOE_GROUPS))
                for j in range(per_group)]
    p1, j1 = _argmax_first(in_group)
    p2, j2 = _argmax_first([jnp.where(j1 == j, -1.0, in_group[j]) for j in range(per_group)])
    e1 = best_group * per_group + j1
    e2 = best_group * per_group + j2
    w1 = p1 / (p1 + p2)
    w2 = p2 / (p1 + p2)

    erow = lax.broadcasted_iota(I32, (n_experts, tr), 0)
    hit1 = erow == e1
    hit2 = erow == e2
    onehot = hit1.astype(F32) + hit2.astype(F32)
    before = cnt_ref[:, 0:1] + jnp.dot(onehot.astype(BF16), upper_ref[...], preferred_element_type=F32)
    rank1 = jnp.sum(jnp.where(hit1, before, 0.0), axis=0, keepdims=True).astype(I32)
    rank2 = jnp.sum(jnp.where(hit2, before, 0.0), axis=0, keepdims=True).astype(I32)
    cnt_ref[...] += jnp.sum(onehot, axis=1, keepdims=True)

    rrow = lax.broadcasted_iota(I32, (SUBLANES, tr), 0)
    route_ref[...] = jnp.where(rrow == 0, e1, jnp.where(rrow == 1, e2, jnp.where(rrow == 2, rank1, rank2)))
    wrow = lax.broadcasted_iota(I32, (LANES, tr), 0)
    wts_ref[...] = jnp.where(wrow == 0, w1, jnp.where(wrow == 1, w2, 0.0)).T


def moe_router(logits, br, n_experts, tr=1024):
    t = logits.shape[0]
    tr = _tile(t, tr)
    return pl.pallas_call(
        functools.partial(_router_kernel, n_experts=n_experts),
        grid=(t // tr,),
        in_specs=[pl.BlockSpec((tr, LANES), lambda i: (i, 0)), pl.BlockSpec((LANES, 1), lambda i: (0, 0))],
        out_specs=[
            pl.BlockSpec((SUBLANES, tr), lambda i: (0, i)),
            pl.BlockSpec((tr, LANES), lambda i: (i, 0)),
            pl.BlockSpec((n_experts, LANES), lambda i: (0, 0)),
        ],
        out_shape=[
            jax.ShapeDtypeStruct((SUBLANES, t), I32),
            jax.ShapeDtypeStruct((t, LANES), F32),
            jax.ShapeDtypeStruct((n_experts, LANES), F32),
        ],
        scratch_shapes=[pltpu.VMEM((tr, tr), BF16)],
        compiler_params=_cparams(("arbitrary",), 32),
        name="moe_router",
    )(logits, br)


def _row_copy(src_ref, src_row, dst_ref, dst_row, sem):
    return pltpu.make_async_copy(
        src_ref.at[pl.ds(pl.multiple_of(src_row * ROW_SLAB, ROW_SLAB), ROW_SLAB), :],
        dst_ref.at[pl.ds(pl.multiple_of(dst_row * ROW_SLAB, ROW_SLAB), ROW_SLAB), :],
        sem)


def _rows_wait(src_ref, dst_ref, n_rows, sem):
    span = pl.ds(0, n_rows * ROW_SLAB)
    pltpu.make_async_copy(src_ref.at[span, :], dst_ref.at[span, :], sem).wait()


def _dispatch_kernel(pos_ref, xg_ref, init_ref, xs_ref, sem, *, td):
    del init_ref

    def issue(t, carry):
        for slot in range(2):
            _row_copy(xg_ref, t, xs_ref, pos_ref[0, 0, slot * td + t], sem).start()
        return carry

    lax.fori_loop(0, td, issue, 0, unroll=4)
    for _ in range(2):
        _rows_wait(xg_ref, xs_ref, td, sem)


def moe_dispatch(xg, pos_blocks, n_sorted_rows, td):
    t = xg.shape[0] // ROW_SLAB
    init = jnp.zeros((n_sorted_rows * ROW_SLAB, LANES), U32)
    return pl.pallas_call(
        functools.partial(_dispatch_kernel, td=td),
        grid=(t // td,),
        in_specs=[
            pl.BlockSpec((1, 1, 2 * td), lambda i: (i, 0, 0), memory_space=pltpu.SMEM),
            pl.BlockSpec((td * ROW_SLAB, LANES), lambda i: (i, 0)),
            pl.BlockSpec(memory_space=pl.ANY),
        ],
        out_specs=pl.BlockSpec(memory_space=pl.ANY),
        out_shape=jax.ShapeDtypeStruct(init.shape, U32),
        scratch_shapes=[pltpu.SemaphoreType.DMA],
        input_output_aliases={2: 0},
        compiler_params=_cparams(("arbitrary",), 32),
        name="moe_dispatch",
    )(pos_blocks, xg, init)


def _expert_kernel(te_ref, nu_ref, xs_ref, wgu_ref, wd_ref, y_ref, *, tm):
    del te_ref
    i = pl.program_id(0)

    @pl.when(i < nu_ref[0])
    def _():
        half = wgu_ref.shape[0] // 2
        ff = wd_ref.shape[0]
        x_lo, x_hi = _unpack_rows(xs_ref, 0, tm, BF16)
        gu = (jnp.dot(x_lo, wgu_ref[:half, :], preferred_element_type=F32)
              + jnp.dot(x_hi, wgu_ref[half:, :], preferred_element_type=F32))
        gate, up = gu[:, :ff], gu[:, ff:]
        hid = (gate / (1.0 + jnp.exp(-gate)) * up).astype(BF16)
        _pack_rows(jnp.dot(hid, wd_ref[...], preferred_element_type=F32), y_ref, tm)

    @pl.when(i >= nu_ref[0])
    def _():
        y_ref[...] = jnp.zeros_like(y_ref)


def moe_experts(xs, wgu, wd, layer, tile_expert, n_used, tm):
    n_tiles = xs.shape[0] // (tm * ROW_SLAB)
    _, _, d, ff2 = wgu.shape
    rows = pl.BlockSpec((tm * ROW_SLAB, LANES), lambda i, te, nu: (i, 0))
    return pl.pallas_call(
        functools.partial(_expert_kernel, tm=tm),
        grid_spec=pltpu.PrefetchScalarGridSpec(
            num_scalar_prefetch=2,
            grid=(n_tiles,),
            in_specs=[
                rows,
                pl.BlockSpec((None, None, d, ff2), lambda i, te, nu: (layer, te[i], 0, 0)),
                pl.BlockSpec((None, None, ff2 // 2, d), lambda i, te, nu: (layer, te[i], 0, 0)),
            ],
            out_specs=rows,
        ),
        out_shape=jax.ShapeDtypeStruct(xs.shape, U32),
        compiler_params=_cparams(("arbitrary",), 56),
        name="moe_experts",
    )(tile_expert, n_used, xs, wgu, wd)


def _combine_ln_kernel(pos_ref, pos_next_ref, wts_ref, x_ref, ys_ref, g_ref, b_ref, of_ref, ob_ref, buf_ref, sem,
                       *, tc, alpha):
    i = pl.program_id(0)
    cur = i % 2

    def start_gather(p_ref, b):
        def issue(t, carry):
            for slot in range(2):
                _row_copy(ys_ref, p_ref[0, 0, slot * tc + t], buf_ref.at[b], slot * tc + t, sem.at[b]).start()
            return carry

        lax.fori_loop(0, tc, issue, 0, unroll=4)

    @pl.when(i == 0)
    def _():
        start_gather(pos_ref, 0)

    @pl.when(i + 1 < pl.num_programs(0))
    def _():
        start_gather(pos_next_ref, 1 - cur)

    rows_in = buf_ref.at[cur]
    _rows_wait(ys_ref, rows_in, 2 * tc, sem.at[cur])

    def group(gi, carry):
        r0 = pl.multiple_of(gi * COMBINE_ROWS, COMBINE_ROWS)
        rows = pl.ds(r0, COMBINE_ROWS)
        y = alpha * x_ref[rows, :]
        for slot in range(2):
            lo, hi = _unpack_rows(rows_in, (slot * tc + r0) * ROW_SLAB, COMBINE_ROWS, F32)
            y = y + wts_ref[rows, slot:slot + 1] * jnp.concatenate([lo, hi], axis=1)
        yn = _layer_norm(y, g_ref[...], b_ref[...])
        of_ref[rows, :] = yn
        ob_ref[rows, :] = yn.astype(BF16)
        return carry

    lax.fori_loop(0, tc // COMBINE_ROWS, group, 0, unroll=2)


def moe_combine_layer_norm(x, ys, pos_blocks, wts, g, b, alpha, tc):
    t, d = x.shape
    n = t // tc
    row = pl.BlockSpec((tc, d), lambda i: (i, 0))
    vec = pl.BlockSpec((1, d), lambda i: (0, 0))
    return pl.pallas_call(
        functools.partial(_combine_ln_kernel, tc=tc, alpha=alpha),
        grid=(n,),
        in_specs=[
            pl.BlockSpec((1, 1, 2 * tc), lambda i: (i, 0, 0), memory_space=pltpu.SMEM),
            pl.BlockSpec((1, 1, 2 * tc), lambda i: (jnp.minimum(i + 1, n - 1), 0, 0), memory_space=pltpu.SMEM),
            pl.BlockSpec((tc, LANES), lambda i: (i, 0)),
            row,
            pl.BlockSpec(memory_space=pl.ANY),
            vec,
            vec,
        ],
        out_specs=[row, row],
        out_shape=[jax.ShapeDtypeStruct((t, d), F32), jax.ShapeDtypeStruct((t, d), BF16)],
        scratch_shapes=[pltpu.VMEM((2, 2 * tc * ROW_SLAB, LANES), U32), pltpu.SemaphoreType.DMA((2,))],
        compiler_params=_cparams(("arbitrary",), 48),
        name="moe_combine_layer_norm",
    )(pos_blocks, pos_blocks, wts, x, ys, g.reshape(1, d), b.reshape(1, d))


def _pos_blocks(pos, tile):
    t = pos.shape[1]
    return pos.reshape(2, t // tile, tile).transpose(1, 0, 2).reshape(t // tile, 1, 2 * tile)


def grouped_moe_layer_norm(xf, xg, logits, br, wgu, wd, layer, g, b, alpha, tm=256, td=512, tc=256):
    t, d = xf.shape
    n_experts = wgu.shape[1]
    route, wts, counts = moe_router(logits, br, n_experts)
    cnt = counts[:, 0].astype(I32)
    padded = (cnt + tm - 1) // tm * tm
    ends = jnp.cumsum(padded)
    offs = ends - padded
    expert_ids = jnp.arange(n_experts, dtype=I32)
    pos = jnp.sum(jnp.where(route[0:2, :, None] == expert_ids, offs, 0), axis=-1) + route[2:4]
    n_tiles = (2 * t) // tm + n_experts
    tile_starts = jnp.arange(n_tiles, dtype=I32) * tm
    tile_expert = jnp.minimum(jnp.sum((ends[None, :] <= tile_starts[:, None]).astype(I32), axis=1), n_experts - 1)
    n_used = (ends[-1:] // tm).astype(I32)
    xs = moe_dispatch(xg, _pos_blocks(pos, td), n_tiles * tm, td)
    ys = moe_experts(xs, wgu, wd, layer, tile_expert, n_used, tm)
    return moe_combine_layer_norm(xf, ys, _pos_blocks(pos, tc), wts, g, b, alpha, tc)


def kernel(x, mem, gla_w_in, gla_w_gate_down, gla_w_gate_up, gla_b_gate, gla_norm_g, gla_w_o, att_w_qkv, att_rel_bias, att_w_o, mx_w_q, mx_w_kv, mx_w_o, router_w, router_b, moe_w_gu, moe_w_down, ln_g, ln_b):
    bsz, seq, d = x.shape
    t = bsz * seq
    depth = ln_g.shape[0]
    alpha = (2.0 * depth) ** 0.25
    n_experts = router_w.shape[1]
    n_gla = gla_w_in.shape[0]
    hk = gla_w_gate_up.shape[2] // GLA_HEADS
    hv = gla_norm_g.shape[1] // GLA_HEADS
    rank = gla_w_gate_down.shape[2]
    mem_len = mem.shape[1]

    xf = x.reshape(t, d)
    xb = xf.astype(BF16)
    mem_b = mem.reshape(bsz * mem_len, d).astype(BF16)
    wr = jnp.pad(router_w, ((0, 0), (0, LANES - n_experts)))
    wr_hi = wr.astype(BF16)
    wr2 = jnp.concatenate([wr_hi, (wr - wr_hi.astype(F32)).astype(BF16)], axis=1)
    br = jnp.pad(router_b, (0, LANES - n_experts)).reshape(LANES, 1)
    w_in, w_go, w_qkv, w_ao = (w.astype(BF16) for w in (gla_w_in, gla_w_o, att_w_qkv, att_w_o))
    w_mq, w_mkv, w_mo = (w.astype(BF16) for w in (mx_w_q, mx_w_kv, mx_w_o))
    w_gu, w_dn = moe_w_gu.astype(BF16), moe_w_down.astype(BF16)
    w_gd = jnp.pad(gla_w_gate_down, ((0, 0), (0, 0), (0, LANES - rank))).astype(BF16)
    w_gup = jnp.pad(gla_w_gate_up.reshape(n_gla, rank, GLA_HEADS, hk).transpose(0, 2, 1, 3),
                    ((0, 0), (0, 0), (0, LANES - rank), (0, 0))).astype(BF16)
    b_gate = gla_b_gate.reshape(n_gla, GLA_HEADS, 1, hk)
    n_gain = gla_norm_g.reshape(n_gla, GLA_HEADS, 1, hv)

    for i in range(depth):
        j = i // 2
        if i % 2 == 0:
            z = matmul(xb, w_in, j, BF16)
            gd = matmul(xb, w_gd, j, F32)
            o = gla_core(z.reshape(bsz, seq, -1), gd.reshape(bsz, seq, LANES), w_gup, b_gate, n_gain, j)
            h = matmul(o.reshape(t, -1), w_go, j, BF16)
        else:
            qkv = matmul(xb, w_qkv, j, BF16)
            o = band_attention_core(qkv.reshape(bsz, seq, -1), _rel_bias(att_rel_bias[j]))
            h = matmul(o.reshape(t, -1), w_ao, j, BF16)
        xf, q = residual_layer_norm_proj(xf, h, ln_g[i, 0], ln_b[i, 0], w_mq, i, alpha)

        kv = matmul(mem_b, w_mkv, i, BF16)
        h = memory_attention_proj(q, kv.reshape(bsz, mem_len, -1), w_mo, i)
        xf, xg, logits = residual_layer_norm_route(xf, h, ln_g[i, 1], ln_b[i, 1], wr2, alpha)

        xf, xb = grouped_moe_layer_norm(xf, xg, logits, br, w_gu, w_dn, i, ln_g[i, 2], ln_b[i, 2], alpha)
    return xf.reshape(bsz, seq, d)
```

```python
import functools

import jax
import jax.numpy as jnp
import numpy as np
from jax import lax
from jax.experimental import pallas as pl
from jax.experimental.pallas import tpu as pltpu

F32 = jnp.float32
BF16 = jnp.bfloat16
U32 = jnp.uint32
I32 = jnp.int32

CHUNK = 64
LN_EPS = 1e-5
GLA_HEADS = 4
GLA_GATE_TAU = 16.0
GLA_GROUP = 4
ATT_HEAD_DIM = 128
ATT_LEFT_CHUNKS = 8
ATT_BAND = (ATT_LEFT_CHUNKS + 1) * CHUNK
ATT_REL_CLIP = 256
ATT_QUERY_CHUNKS = 4
ATT_QB = ATT_QUERY_CHUNKS * CHUNK
ATT_WIN = (ATT_LEFT_CHUNKS + ATT_QUERY_CHUNKS) * CHUNK
MEM_HEADS = 4
MOE_GROUPS = 4

LANES = 128
SUBLANES = 8

ROW_SLAB = 16
COMBINE_ROWS = 16

_TRANS_B = (((1,), (1,)), ((), ()))
_TRANS_A = (((0,), (0,)), ((), ()))


def _cparams(sem, vmem_mb):
    return pltpu.CompilerParams(dimension_semantics=sem, vmem_limit_bytes=vmem_mb * 1024 * 1024)


def _tile(dim, pref):
    t = min(dim, pref)
    assert dim % t == 0, (dim, pref)
    return t


def _matmul_kernel(a_ref, w_ref, o_ref):
    o_ref[...] = jnp.dot(a_ref[...], w_ref[...], preferred_element_type=F32).astype(o_ref.dtype)


def matmul(a, w, layer, out_dtype, tm=1024, tn=1024):
    m, k = a.shape
    _, k2, n = w.shape
    assert k == k2
    tm, tn = _tile(m, tm), _tile(n, tn)
    return pl.pallas_call(
        _matmul_kernel,
        grid=(m // tm, n // tn),
        in_specs=[pl.BlockSpec((tm, k), lambda i, j: (i, 0)), pl.BlockSpec((None, k, tn), lambda i, j: (layer, 0, j))],
        out_specs=pl.BlockSpec((tm, tn), lambda i, j: (i, j)),
        out_shape=jax.ShapeDtypeStruct((m, n), out_dtype),
        compiler_params=_cparams(("parallel", "parallel"), 56),
        name="matmul",
    )(a, w)


def _layer_norm(y, g, b):
    mu = jnp.mean(y, axis=-1, keepdims=True)
    yc = y - mu
    var = jnp.mean(yc * yc, axis=-1, keepdims=True)
    return yc * lax.rsqrt(var + LN_EPS) * g + b


def _round_bf16_bits(v):
    return lax.bitcast_convert_type(v.astype(BF16).astype(F32), U32)


def _pack_rows(yn, o_ref, rows):
    half = yn.shape[1] // 2
    packed = (_round_bf16_bits(yn[:, :half]) >> 16) | (_round_bf16_bits(yn[:, half:]) & jnp.uint32(0xFFFF0000))
    for s in range(ROW_SLAB):
        o_ref[pl.ds(s, rows, stride=ROW_SLAB), :] = packed[:, s * LANES:(s + 1) * LANES]


def _unpack_rows(ref, base, rows, dtype):
    lo, hi = [], []
    for s in range(ROW_SLAB):
        u = ref[pl.ds(base + s, rows, stride=ROW_SLAB), :]
        lo.append(lax.bitcast_convert_type(u << 16, F32).astype(dtype))
        hi.append(lax.bitcast_convert_type(u & jnp.uint32(0xFFFF0000), F32).astype(dtype))
    return jnp.concatenate(lo, axis=1), jnp.concatenate(hi, axis=1)


def _res_ln_proj_kernel(x_ref, h_ref, g_ref, b_ref, wq_ref, of_ref, oq_ref, *, alpha):
    yn = _layer_norm(alpha * x_ref[...] + h_ref[...].astype(F32), g_ref[...], b_ref[...])
    of_ref[...] = yn
    oq_ref[...] = jnp.dot(yn.astype(BF16), wq_ref[...], preferred_element_type=F32).astype(oq_ref.dtype)


def residual_layer_norm_proj(x, h, g, b, wq, layer, alpha, tm=256):
    t, d = x.shape
    nq = wq.shape[2]
    tm = _tile(t, tm)
    row = pl.BlockSpec((tm, d), lambda i: (i, 0))
    vec = pl.BlockSpec((1, d), lambda i: (0, 0))
    return pl.pallas_call(
        functools.partial(_res_ln_proj_kernel, alpha=alpha),
        grid=(t // tm,),
        in_specs=[row, row, vec, vec, pl.BlockSpec((None, d, nq), lambda i: (layer, 0, 0))],
        out_specs=[row, pl.BlockSpec((tm, nq), lambda i: (i, 0))],
        out_shape=[jax.ShapeDtypeStruct((t, d), F32), jax.ShapeDtypeStruct((t, nq), BF16)],
        compiler_params=_cparams(("parallel",), 56),
        name="residual_layer_norm_proj",
    )(x, h, g.reshape(1, d), b.reshape(1, d), wq)


def _memory_attention(q_ref, kv_ref):
    width = q_ref.shape[1]
    head_dim = width // MEM_HEADS
    scale = float(head_dim) ** -0.5
    outs = []
    for h in range(MEM_HEADS):
        cols = slice(h * head_dim, (h + 1) * head_dim)
        sc = lax.dot_general(q_ref[:, cols], kv_ref[:, cols], _TRANS_B, preferred_element_type=F32) * scale
        p = jnp.exp(sc - jnp.max(sc, axis=-1, keepdims=True))
        den = jnp.sum(p, axis=-1, keepdims=True)
        vh = kv_ref[:, width + h * head_dim:width + (h + 1) * head_dim]
        outs.append((jnp.dot(p.astype(BF16), vh, preferred_element_type=F32) / den).astype(BF16))
    return jnp.concatenate(outs, axis=1)


def _mem_attn_proj_kernel(q_ref, kv_ref, wo_ref, h_ref):
    h_ref[...] = jnp.dot(_memory_attention(q_ref, kv_ref), wo_ref[...], preferred_element_type=F32).astype(h_ref.dtype)


def memory_attention_proj(q, kv, wo, layer, tm=512):
    t, width = q.shape
    bsz, m, w2 = kv.shape
    d = wo.shape[2]
    tm = _tile(t // bsz, tm)
    tiles_per_batch = t // bsz // tm
    assert w2 == 2 * width
    return pl.pallas_call(
        _mem_attn_proj_kernel,
        grid=(t // tm,),
        in_specs=[pl.BlockSpec((tm, width), lambda i: (i, 0)),
                  pl.BlockSpec((None, m, w2), lambda i: (i // tiles_per_batch, 0, 0)),
                  pl.BlockSpec((None, width, d), lambda i: (layer, 0, 0))],
        out_specs=pl.BlockSpec((tm, d), lambda i: (i, 0)),
        out_shape=jax.ShapeDtypeStruct((t, d), BF16),
        compiler_params=_cparams(("parallel",), 48),
        name="memory_attention_proj",
    )(q, kv, wo)


def _res_ln_route_kernel(x_ref, h_ref, g_ref, b_ref, wr_ref, of_ref, og_ref, lg_ref, *, alpha):
    yn = _layer_norm(alpha * x_ref[...] + h_ref[...].astype(F32), g_ref[...], b_ref[...])
    of_ref[...] = yn
    _pack_rows(yn, og_ref, yn.shape[0])
    y_hi = yn.astype(BF16)
    y_lo = (yn - y_hi.astype(F32)).astype(BF16)
    both = jnp.dot(y_hi, wr_ref[...], preferred_element_type=F32)
    lg_ref[...] = both[:, :LANES] + both[:, LANES:] + jnp.dot(y_lo, wr_ref[:, :LANES], preferred_element_type=F32)


def residual_layer_norm_route(x, h, g, b, router_w2, alpha, tm=256):
    t, d = x.shape
    tm = _tile(t, tm)
    assert d == 2 * ROW_SLAB * LANES
    row = pl.BlockSpec((tm, d), lambda i: (i, 0))
    vec = pl.BlockSpec((1, d), lambda i: (0, 0))
    return pl.pallas_call(
        functools.partial(_res_ln_route_kernel, alpha=alpha),
        grid=(t // tm,),
        in_specs=[row, row, vec, vec, pl.BlockSpec((d, 2 * LANES), lambda i: (0, 0))],
        out_specs=[row, pl.BlockSpec((tm * ROW_SLAB, LANES), lambda i: (i, 0)), pl.BlockSpec((tm, LANES), lambda i: (i, 0))],
        out_shape=[jax.ShapeDtypeStruct((t, d), F32), jax.ShapeDtypeStruct((t * ROW_SLAB, LANES), U32),
                   jax.ShapeDtypeStruct((t, LANES), F32)],
        compiler_params=_cparams(("parallel",), 48),
        name="residual_layer_norm_route",
    )(x, h, g.reshape(1, d), b.reshape(1, d), router_w2)


def _gla_kernel(q_ref, k_ref, v_ref, g_ref, gd_ref, wu_ref, bg_ref, ng_ref, o_ref, st_ref, *, n_groups, q_scale):
    rows_g = GLA_GROUP * CHUNK

    @pl.when(pl.program_id(2) == 0)
    def _():
        st_ref[...] = jnp.zeros_like(st_ref)

    ri = lax.broadcasted_iota(I32, (rows_g, rows_g), 0)
    ci = lax.broadcasted_iota(I32, (rows_g, rows_g), 1)
    chunk_gap = ri // CHUNK - ci // CHUNK
    causal = (chunk_gap == 0) & (ri >= ci)
    tri = causal.astype(BF16)

    def stack(rows_of_chunk):
        return jnp.concatenate([jnp.broadcast_to(r, (CHUNK, r.shape[1])) for r in rows_of_chunk], axis=0)

    def group(gi, carry):
        rows = pl.ds(pl.multiple_of(gi * rows_g, rows_g), rows_g)
        q = q_ref[rows, :].astype(F32) * q_scale
        k = k_ref[rows, :].astype(F32)
        v = v_ref[rows, :]
        gate = jnp.dot(gd_ref[rows, :].astype(BF16), wu_ref[...], preferred_element_type=F32) + bg_ref[...]
        la = (jnp.minimum(gate, 0.0) - jnp.log(1.0 + jnp.exp(-jnp.abs(gate)))) * (1.0 / GLA_GATE_TAU)
        hi = la.astype(BF16)
        r1 = la - hi.astype(F32)
        mid = r1.astype(BF16)
        lo = (r1 - mid.astype(F32)).astype(BF16)
        bc = (jnp.dot(tri, hi, preferred_element_type=F32) + jnp.dot(tri, mid, preferred_element_type=F32)
              + jnp.dot(tri, lo, preferred_element_type=F32))
        bl = [bc[(c + 1) * CHUNK - 1:(c + 1) * CHUNK, :] for c in range(GLA_GROUP)]
        zero = jnp.zeros_like(bl[0])
        before = [sum(bl[:c], zero) for c in range(GLA_GROUP)]
        after = [sum(bl[c + 1:], zero) for c in range(GLA_GROUP)]
        q_e = q * jnp.exp(bc)
        k_e = k * jnp.exp(-bc)
        k_d = k_e * stack([jnp.exp(r) for r in bl])
        k_eb = k_e.astype(BF16)
        k_db = k_d.astype(BF16)
        att = jnp.where(causal, lax.dot_general(q_e.astype(BF16), k_eb, _TRANS_B, preferred_element_type=F32), 0.0)
        for gap in range(1, GLA_GROUP):
            between = [jnp.exp(sum(bl[c - gap + 1:c], zero)) if c >= gap else zero for c in range(GLA_GROUP)]
            q_gap = q_e if gap == 1 else q_e * stack(between)
            att = att + jnp.where(chunk_gap == gap,
                                  lax.dot_general(q_gap.astype(BF16), k_db, _TRANS_B, preferred_element_type=F32), 0.0)
        st = st_ref[...]
        q_st = (q_e * stack([jnp.exp(r) for r in before])).astype(BF16)
        o = jnp.dot(att.astype(BF16), v, preferred_element_type=F32) + lax.dot_general(
            q_st, st.astype(BF16), _TRANS_B, preferred_element_type=F32)
        k_up = (k_d * stack([jnp.exp(r) for r in after])).astype(BF16)
        st_ref[...] = st * jnp.exp(before[-1] + bl[-1]) + lax.dot_general(v, k_up, _TRANS_A, preferred_element_type=F32)
        on = o * lax.rsqrt(jnp.mean(o * o, axis=-1, keepdims=True) + LN_EPS) * ng_ref[...]
        g = g_ref[rows, :].astype(F32)
        o_ref[rows, :] = (on * (g / (1.0 + jnp.exp(-g)))).astype(o_ref.dtype)
        return carry

    lax.fori_loop(0, n_groups, group, 0, unroll=2)


def gla_core(z, gd, wu, bg, ng, layer, rows_per_step=512):
    b, s, zc = z.shape
    hk = wu.shape[3]
    hv = ng.shape[3]
    assert zc == GLA_HEADS * (2 * hk + 2 * hv) and hv == 2 * hk
    rb = _tile(s, rows_per_step)
    assert rb % (2 * GLA_GROUP * CHUNK) == 0
    kq = GLA_HEADS
    kv = 2 * GLA_HEADS * hk // hv
    kg = kv + GLA_HEADS
    return pl.pallas_call(
        functools.partial(_gla_kernel, n_groups=rb // (GLA_GROUP * CHUNK), q_scale=float(hk) ** -0.5),
        grid=(b, GLA_HEADS, s // rb),
        in_specs=[
            pl.BlockSpec((None, rb, hk), lambda bi, h, r: (bi, r, h)),
            pl.BlockSpec((None, rb, hk), lambda bi, h, r: (bi, r, kq + h)),
            pl.BlockSpec((None, rb, hv), lambda bi, h, r: (bi, r, kv + h)),
            pl.BlockSpec((None, rb, hv), lambda bi, h, r: (bi, r, kg + h)),
            pl.BlockSpec((None, rb, LANES), lambda bi, h, r: (bi, r, 0)),
            pl.BlockSpec((None, None, LANES, hk), lambda bi, h, r: (layer, h, 0, 0)),
            pl.BlockSpec((None, None, 1, hk), lambda bi, h, r: (layer, h, 0, 0)),
            pl.BlockSpec((None, None, 1, hv), lambda bi, h, r: (layer, h, 0, 0)),
        ],
        out_specs=pl.BlockSpec((None, rb, hv), lambda bi, h, r: (bi, r, h)),
        out_shape=jax.ShapeDtypeStruct((b, s, GLA_HEADS * hv), BF16),
        scratch_shapes=[pltpu.VMEM((hv, hk), F32)],
        compiler_params=_cparams(("parallel", "parallel", "arbitrary"), 40),
        name="gla_core",
    )(z, z, z, z, gd, wu, bg, ng)


def _band_attn_kernel(q_ref, k_ref, v_ref, bias_ref, o_ref, kp_ref, vp_ref, *, n_blocks, scale):
    pad = ATT_LEFT_CHUNKS * CHUNK
    s = n_blocks * ATT_QB
    kp_ref[pl.ds(0, pad), :] = jnp.zeros((pad, ATT_HEAD_DIM), kp_ref.dtype)
    vp_ref[pl.ds(0, pad), :] = jnp.zeros((pad, ATT_HEAD_DIM), vp_ref.dtype)
    kp_ref[pl.ds(pad, s), :] = k_ref[...]
    vp_ref[pl.ds(pad, s), :] = v_ref[...]
    last_tile = bias_ref.shape[0] - 1

    def block(i, carry):
        start = pl.multiple_of(i * ATT_QB, ATT_QB)
        q = (q_ref[pl.ds(start, ATT_QB), :].astype(F32) * scale).astype(BF16)
        kb = kp_ref[pl.ds(start, ATT_WIN), :]
        vb = vp_ref[pl.ds(start, ATT_WIN), :]
        sc = lax.dot_general(q, kb, _TRANS_B, preferred_element_type=F32) + bias_ref[jnp.minimum(i, last_tile)]
        p = jnp.exp(sc - jnp.max(sc, axis=-1, keepdims=True))
        den = jnp.sum(p, axis=-1, keepdims=True)
        o = jnp.dot(p.astype(BF16), vb, preferred_element_type=F32) / den
        o_ref[pl.ds(start, ATT_QB), :] = o.astype(o_ref.dtype)
        return carry

    lax.fori_loop(0, n_blocks, block, 0, unroll=4)


def band_attention_core(qkv, bias):
    b, s, c3 = qkv.shape
    h, n_tiles = bias.shape[:2]
    assert c3 == 3 * h * ATT_HEAD_DIM and s % (4 * ATT_QB) == 0
    blk = lambda off: pl.BlockSpec((None, s, ATT_HEAD_DIM), lambda bi, hi: (bi, 0, off + hi))
    return pl.pallas_call(
        functools.partial(_band_attn_kernel, n_blocks=s // ATT_QB, scale=float(ATT_HEAD_DIM) ** -0.5),
        grid=(b, h),
        in_specs=[blk(0), blk(h), blk(2 * h),
                  pl.BlockSpec((None, n_tiles, ATT_QB, ATT_WIN), lambda bi, hi: (hi, 0, 0, 0))],
        out_specs=blk(0),
        out_shape=jax.ShapeDtypeStruct((b, s, h * ATT_HEAD_DIM), BF16),
        scratch_shapes=[pltpu.VMEM((s + ATT_LEFT_CHUNKS * CHUNK, ATT_HEAD_DIM), BF16)] * 2,
        compiler_params=_cparams(("parallel", "parallel"), 40),
        name="band_attention_core",
    )(qkv, qkv, qkv, bias)


def _rel_bias(rel_table):
    h = rel_table.shape[0]
    pad = ATT_LEFT_CHUNKS * CHUNK
    assert ATT_REL_CLIP >= CHUNK and 2 * ATT_REL_CLIP + 1 >= ATT_BAND - ATT_REL_CLIP
    rev = rel_table[:, ::-1].astype(F32)
    lead = pad - ATT_REL_CLIP + CHUNK - 1
    ext = jnp.concatenate([jnp.broadcast_to(rev[:, :1], (h, lead)), rev[:, :ATT_BAND - ATT_REL_CLIP]], axis=1)
    band = jnp.stack([ext[:, CHUNK - 1 - i:CHUNK - 1 - i + ATT_BAND] for i in range(CHUNK)], axis=1)
    neg = jnp.full((h, CHUNK, ATT_WIN), -jnp.inf, F32)
    window = jnp.concatenate(
        [lax.dynamic_update_slice(neg, band, (0, 0, c * CHUNK)) for c in range(ATT_QUERY_CHUNKS)], axis=1)
    n_tiles = -(-pad // ATT_QB) + 1
    slot = np.arange(ATT_WIN)[None, :]
    keep = np.stack([np.broadcast_to(slot >= pad - n * ATT_QB, (ATT_QB, ATT_WIN)) for n in range(n_tiles)])
    return jnp.where(keep[None], window[:, None], -jnp.inf)


def _top2_of4(a, b, c, d):
    hi1, lo1 = jnp.maximum(a, b), jnp.minimum(a, b)
    hi2, lo2 = jnp.maximum(c, d), jnp.minimum(c, d)
    return jnp.maximum(hi1, hi2) + jnp.maximum(jnp.minimum(hi1, hi2), jnp.maximum(lo1, lo2))


def _argmax_first(vals):
    best, idx = vals[0], jnp.zeros(vals[0].shape, I32)
    for j in range(1, len(vals)):
        better = vals[j] > best
        best = jnp.where(better, vals[j], best)
        idx = jnp.where(better, j, idx)
    return best, idx


def _router_kernel(lg_ref, br_ref, route_ref, wts_ref, cnt_ref, upper_ref, *, n_experts):
    tr = lg_ref.shape[0]
    per_group = n_experts // MOE_GROUPS

    @pl.when(pl.program_id(0) == 0)
    def _():
        cnt_ref[...] = jnp.zeros_like(cnt_ref)
        earlier = lax.broadcasted_iota(I32, (tr, tr), 0) < lax.broadcasted_iota(I32, (tr, tr), 1)
        upper_ref[...] = earlier.astype(BF16)

    lt = lg_ref[...].T[:n_experts, :] + br_ref[:n_experts, :]
    ex = jnp.exp(lt - jnp.max(lt, axis=0, keepdims=True))
    probs = ex / jnp.sum(ex, axis=0, keepdims=True)
    p = [probs[e:e + 1, :] for e in range(n_experts)]

    assert per_group == 4
    scores = [_top2_of4(*p[per_group * g:per_group * (g + 1)]) for g in range(MOE_GROUPS)]
    _, best_group = _argmax_first(scores)
    in_group = [sum(jnp.where(best_group == g, p[per_group * g + j], 0.0) for g in range(MOE_GROUPS))
                for j in range(per_group)]
    p1, j1 = _argmax_first(in_group)
    p2, j2 = _argmax_first([jnp.where(j1 == j, -1.0, in_group[j]) for j in range(per_group)])
    e1 = best_group * per_group + j1
    e2 = best_group * per_group + j2
    w1 = p1 / (p1 + p2)
    w2 = p2 / (p1 + p2)

    erow = lax.broadcasted_iota(I32, (n_experts, tr), 0)
    hit1 = erow == e1
    hit2 = erow == e2
    onehot = hit1.astype(F32) + hit2.astype(F32)
    before = cnt_ref[:, 0:1] + jnp.dot(onehot.astype(BF16), upper_ref[...], preferred_element_type=F32)
    rank1 = jnp.sum(jnp.where(hit1, before, 0.0), axis=0, keepdims=True).astype(I32)
    rank2 = jnp.sum(jnp.where(hit2, before, 0.0), axis=0, keepdims=True).astype(I32)
    cnt_ref[...] += jnp.sum(onehot, axis=1, keepdims=True)

    rrow = lax.broadcasted_iota(I32, (SUBLANES, tr), 0)
    route_ref[...] = jnp.where(rrow == 0, e1, jnp.where(rrow == 1, e2, jnp.where(rrow == 2, rank1, rank2)))
    wrow = lax.broadcasted_iota(I32, (LANES, tr), 0)
    wts_ref[...] = jnp.where(wrow == 0, w1, jnp.where(wrow == 1, w2, 0.0)).T


def moe_router(logits, br, n_experts, tr=1024):
    t = logits.shape[0]
    tr = _tile(t, tr)
    return pl.pallas_call(
        functools.partial(_router_kernel, n_experts=n_experts),
        grid=(t // tr,),
        in_specs=[pl.BlockSpec((tr, LANES), lambda i: (i, 0)), pl.BlockSpec((LANES, 1), lambda i: (0, 0))],
        out_specs=[
            pl.BlockSpec((SUBLANES, tr), lambda i: (0, i)),
            pl.BlockSpec((tr, LANES), lambda i: (i, 0)),
            pl.BlockSpec((n_experts, LANES), lambda i: (0, 0)),
        ],
        out_shape=[
            jax.ShapeDtypeStruct((SUBLANES, t), I32),
            jax.ShapeDtypeStruct((t, LANES), F32),
            jax.ShapeDtypeStruct((n_experts, LANES), F32),
        ],
        scratch_shapes=[pltpu.VMEM((tr, tr), BF16)],
        compiler_params=_cparams(("arbitrary",), 32),
        name="moe_router",
    )(logits, br)


def _row_copy(src_ref, src_row, dst_ref, dst_row, sem):
    return pltpu.make_async_copy(
        src_ref.at[pl.ds(pl.multiple_of(src_row * ROW_SLAB, ROW_SLAB), ROW_SLAB), :],
        dst_ref.at[pl.ds(pl.multiple_of(dst_row * ROW_SLAB, ROW_SLAB), ROW_SLAB), :],
        sem)


def _rows_wait(src_ref, dst_ref, n_rows, sem):
    span = pl.ds(0, n_rows * ROW_SLAB)
    pltpu.make_async_copy(src_ref.at[span, :], dst_ref.at[span, :], sem).wait()


def _dispatch_kernel(pad_start_ref, pad_count_ref, pos_ref, xg_ref, xs_ref, zero_ref, sem, pad_sem, *, td, n_pads):
    @pl.when(pl.program_id(0) == 0)
    def _():
        zero_ref[...] = jnp.zeros_like(zero_ref)
        for s in range(n_pads):
            def fill(r, carry, s=s):
                _row_copy(zero_ref, 0, xs_ref, pad_start_ref[s] + r, pad_sem).start()
                return carry

            lax.fori_loop(0, pad_count_ref[s], fill, 0)
        for s in range(n_pads):
            def drain(r, carry):
                _row_copy(zero_ref, 0, xs_ref, 0, pad_sem).wait()
                return carry

            lax.fori_loop(0, pad_count_ref[s], drain, 0)

    def issue(t, carry):
        for slot in range(2):
            _row_copy(xg_ref, t, xs_ref, pos_ref[0, 0, slot * td + t], sem).start()
        return carry

    lax.fori_loop(0, td, issue, 0, unroll=4)
    for _ in range(2):
        _rows_wait(xg_ref, xs_ref, td, sem)


def moe_dispatch(xg, pos_blocks, pad_start, pad_count, n_sorted_rows, td):
    t = xg.shape[0] // ROW_SLAB
    return pl.pallas_call(
        functools.partial(_dispatch_kernel, td=td, n_pads=pad_start.shape[0]),
        grid_spec=pltpu.PrefetchScalarGridSpec(
            num_scalar_prefetch=2,
            grid=(t // td,),
            in_specs=[
                pl.BlockSpec((1, 1, 2 * td), lambda i, ps, pc: (i, 0, 0), memory_space=pltpu.SMEM),
                pl.BlockSpec((td * ROW_SLAB, LANES), lambda i, ps, pc: (i, 0)),
            ],
            out_specs=pl.BlockSpec(memory_space=pl.ANY),
            scratch_shapes=[pltpu.VMEM((ROW_SLAB, LANES), U32), pltpu.SemaphoreType.DMA, pltpu.SemaphoreType.DMA],
        ),
        out_shape=jax.ShapeDtypeStruct((n_sorted_rows * ROW_SLAB, LANES), U32),
        compiler_params=_cparams(("arbitrary",), 32),
        name="moe_dispatch",
    )(pad_start, pad_count, pos_blocks, xg)


def _expert_kernel(te_ref, nu_ref, xs_ref, wgu_ref, wd_ref, y_ref, *, tm):
    del te_ref
    i = pl.program_id(0)

    @pl.when(i < nu_ref[0])
    def _():
        half = wgu_ref.shape[0] // 2
        ff = wd_ref.shape[0]
        x_lo, x_hi = _unpack_rows(xs_ref, 0, tm, BF16)
        gu = (jnp.dot(x_lo, wgu_ref[:half, :], preferred_element_type=F32)
              + jnp.dot(x_hi, wgu_ref[half:, :], preferred_element_type=F32))
        gate, up = gu[:, :ff], gu[:, ff:]
        hid = (gate / (1.0 + jnp.exp(-gate)) * up).astype(BF16)
        _pack_rows(jnp.dot(hid, wd_ref[...], preferred_element_type=F32), y_ref, tm)

    @pl.when(i >= nu_ref[0])
    def _():
        y_ref[...] = jnp.zeros_like(y_ref)


def moe_experts(xs, wgu, wd, layer, tile_expert, n_used, tm):
    n_tiles = xs.shape[0] // (tm * ROW_SLAB)
    _, _, d, ff2 = wgu.shape
    rows = pl.BlockSpec((tm * ROW_SLAB, LANES), lambda i, te, nu: (i, 0))
    return pl.pallas_call(
        functools.partial(_expert_kernel, tm=tm),
        grid_spec=pltpu.PrefetchScalarGridSpec(
            num_scalar_prefetch=2,
            grid=(n_tiles,),
            in_specs=[
                rows,
                pl.BlockSpec((None, None, d, ff2), lambda i, te, nu: (layer, te[i], 0, 0)),
                pl.BlockSpec((None, None, ff2 // 2, d), lambda i, te, nu: (layer, te[i], 0, 0)),
            ],
            out_specs=rows,
        ),
        out_shape=jax.ShapeDtypeStruct(xs.shape, U32),
        compiler_params=_cparams(("arbitrary",), 56),
        name="moe_experts",
    )(tile_expert, n_used, xs, wgu, wd)


def _combine_ln_kernel(pos_ref, pos_next_ref, wts_ref, x_ref, ys_ref, g_ref, b_ref, of_ref, ob_ref, buf_ref, sem,
                       *, tc, alpha):
    i = pl.program_id(0)
    cur = i % 2

    def start_gather(p_ref, b):
        def issue(t, carry):
            for slot in range(2):
                _row_copy(ys_ref, p_ref[0, 0, slot * tc + t], buf_ref.at[b], slot * tc + t, sem.at[b]).start()
            return carry

        lax.fori_loop(0, tc, issue, 0, unroll=4)

    @pl.when(i == 0)
    def _():
        start_gather(pos_ref, 0)

    @pl.when(i + 1 < pl.num_programs(0))
    def _():
        start_gather(pos_next_ref, 1 - cur)

    rows_in = buf_ref.at[cur]
    _rows_wait(ys_ref, rows_in, 2 * tc, sem.at[cur])

    def group(gi, carry):
        r0 = pl.multiple_of(gi * COMBINE_ROWS, COMBINE_ROWS)
        rows = pl.ds(r0, COMBINE_ROWS)
        y = alpha * x_ref[rows, :]
        for slot in range(2):
            lo, hi = _unpack_rows(rows_in, (slot * tc + r0) * ROW_SLAB, COMBINE_ROWS, F32)
            y = y + wts_ref[rows, slot:slot + 1] * jnp.concatenate([lo, hi], axis=1)
        yn = _layer_norm(y, g_ref[...], b_ref[...])
        of_ref[rows, :] = yn
        ob_ref[rows, :] = yn.astype(BF16)
        return carry

    lax.fori_loop(0, tc // COMBINE_ROWS, group, 0, unroll=2)


def moe_combine_layer_norm(x, ys, pos_blocks, wts, g, b, alpha, tc):
    t, d = x.shape
    n = t // tc
    row = pl.BlockSpec((tc, d), lambda i: (i, 0))
    vec = pl.BlockSpec((1, d), lambda i: (0, 0))
    return pl.pallas_call(
        functools.partial(_combine_ln_kernel, tc=tc, alpha=alpha),
        grid=(n,),
        in_specs=[
            pl.BlockSpec((1, 1, 2 * tc), lambda i: (i, 0, 0), memory_space=pltpu.SMEM),
            pl.BlockSpec((1, 1, 2 * tc), lambda i: (jnp.minimum(i + 1, n - 1), 0, 0), memory_space=pltpu.SMEM),
            pl.BlockSpec((tc, LANES), lambda i: (i, 0)),
            row,
            pl.BlockSpec(memory_space=pl.ANY),
            vec,
            vec,
        ],
        out_specs=[row, row],
        out_shape=[jax.ShapeDtypeStruct((t, d), F32), jax.ShapeDtypeStruct((t, d), BF16)],
        scratch_shapes=[pltpu.VMEM((2, 2 * tc * ROW_SLAB, LANES), U32), pltpu.SemaphoreType.DMA((2,))],
        compiler_params=_cparams(("arbitrary",), 48),
        name="moe_combine_layer_norm",
    )(pos_blocks, pos_blocks, wts, x, ys, g.reshape(1, d), b.reshape(1, d))


def _pos_blocks(pos, tile):
    t = pos.shape[1]
    return pos.reshape(2, t // tile, tile).transpose(1, 0, 2).reshape(t // tile, 1, 2 * tile)


def grouped_moe_layer_norm(xf, xg, logits, br, wgu, wd, layer, g, b, alpha, tm=256, td=512, tc=256):
    t, d = xf.shape
    n_experts = wgu.shape[1]
    route, wts, counts = moe_router(logits, br, n_experts)
    cnt = counts[:, 0].astype(I32)
    padded = (cnt + tm - 1) // tm * tm
    ends = jnp.cumsum(padded)
    offs = ends - padded
    expert_ids = jnp.arange(n_experts, dtype=I32)
    pos = jnp.sum(jnp.where(route[0:2, :, None] == expert_ids, offs, 0), axis=-1) + route[2:4]
    n_tiles = (2 * t) // tm + n_experts
    tile_starts = jnp.arange(n_tiles, dtype=I32) * tm
    tile_expert = jnp.minimum(jnp.sum((ends[None, :] <= tile_starts[:, None]).astype(I32), axis=1), n_experts - 1)
    n_used = (ends[-1:] // tm).astype(I32)
    pad_start = jnp.concatenate([offs + cnt, ends[-1:]])
    pad_count = jnp.concatenate([padded - cnt, n_tiles * tm - ends[-1:]])
    xs = moe_dispatch(xg, _pos_blocks(pos, td), pad_start, pad_count, n_tiles * tm, td)
    ys = moe_experts(xs, wgu, wd, layer, tile_expert, n_used, tm)
    return moe_combine_layer_norm(xf, ys, _pos_blocks(pos, tc), wts, g, b, alpha, tc)


def kernel(x, mem, gla_w_in, gla_w_gate_down, gla_w_gate_up, gla_b_gate, gla_norm_g, gla_w_o, att_w_qkv, att_rel_bias, att_w_o, mx_w_q, mx_w_kv, mx_w_o, router_w, router_b, moe_w_gu, moe_w_down, ln_g, ln_b):
    bsz, seq, d = x.shape
    t = bsz * seq
    depth = ln_g.shape[0]
    alpha = (2.0 * depth) ** 0.25
    n_experts = router_w.shape[1]
    n_gla = gla_w_in.shape[0]
    hk = gla_w_gate_up.shape[2] // GLA_HEADS
    hv = gla_norm_g.shape[1] // GLA_HEADS
    rank = gla_w_gate_down.shape[2]
    mem_len = mem.shape[1]

    xf = x.reshape(t, d)
    xb = xf.astype(BF16)
    mem_b = mem.reshape(bsz * mem_len, d).astype(BF16)
    wr = jnp.pad(router_w, ((0, 0), (0, LANES - n_experts)))
    wr_hi = wr.astype(BF16)
    wr2 = jnp.concatenate([wr_hi, (wr - wr_hi.astype(F32)).astype(BF16)], axis=1)
    br = jnp.pad(router_b, (0, LANES - n_experts)).reshape(LANES, 1)
    w_in, w_go, w_qkv, w_ao = (w.astype(BF16) for w in (gla_w_in, gla_w_o, att_w_qkv, att_w_o))
    w_mq, w_mkv, w_mo = (w.astype(BF16) for w in (mx_w_q, mx_w_kv, mx_w_o))
    w_gu, w_dn = moe_w_gu.astype(BF16), moe_w_down.astype(BF16)
    w_gd = jnp.pad(gla_w_gate_down, ((0, 0), (0, 0), (0, LANES - rank))).astype(BF16)
    w_gup = jnp.pad(gla_w_gate_up.reshape(n_gla, rank, GLA_HEADS, hk).transpose(0, 2, 1, 3),
                    ((0, 0), (0, 0), (0, LANES - rank), (0, 0))).astype(BF16)
    b_gate = gla_b_gate.reshape(n_gla, GLA_HEADS, 1, hk)
    n_gain = gla_norm_g.reshape(n_gla, GLA_HEADS, 1, hv)

    for i in range(depth):
        j = i // 2
        if i % 2 == 0:
            z = matmul(xb, w_in, j, BF16)
            gd = matmul(xb, w_gd, j, F32)
            o = gla_core(z.reshape(bsz, seq, -1), gd.reshape(bsz, seq, LANES), w_gup, b_gate, n_gain, j)
            h = matmul(o.reshape(t, -1), w_go, j, BF16)
        else:
            qkv = matmul(xb, w_qkv, j, BF16)
            o = band_attention_core(qkv.reshape(bsz, seq, -1), _rel_bias(att_rel_bias[j]))
            h = matmul(o.reshape(t, -1), w_ao, j, BF16)
        xf, q = residual_layer_norm_proj(xf, h, ln_g[i, 0], ln_b[i, 0], w_mq, i, alpha)

        kv = matmul(mem_b, w_mkv, i, BF16)
        h = memory_attention_proj(q, kv.reshape(bsz, mem_len, -1), w_mo, i)
        xf, xg, logits = residual_layer_norm_route(xf, h, ln_g[i, 1], ln_b[i, 1], wr2, alpha)

        xf, xb = grouped_moe_layer_norm(xf, xg, logits, br, w_gu, w_dn, i, ln_g[i, 2], ln_b[i, 2], alpha)
    return xf.reshape(bsz, seq, d)
```

```python
import functools

import jax
import jax.numpy as jnp
import numpy as np
from jax import lax
from jax.experimental import pallas as pl
from jax.experimental.pallas import tpu as pltpu

F32 = jnp.float32
BF16 = jnp.bfloat16
U32 = jnp.uint32
I32 = jnp.int32

CHUNK = 64
LN_EPS = 1e-5
GLA_HEADS = 4
GLA_GATE_TAU = 16.0
GLA_GROUP = 4
ATT_HEAD_DIM = 128
ATT_LEFT_CHUNKS = 8
ATT_BAND = (ATT_LEFT_CHUNKS + 1) * CHUNK
ATT_REL_CLIP = 256
ATT_QUERY_CHUNKS = 4
ATT_QB = ATT_QUERY_CHUNKS * CHUNK
ATT_WIN = (ATT_LEFT_CHUNKS + ATT_QUERY_CHUNKS) * CHUNK
ATT_UNROLL = 8
MEM_HEADS = 4
MOE_GROUPS = 4

LANES = 128
SUBLANES = 8

ROW_SLAB = 16
COMBINE_ROWS = 16

_TRANS_B = (((1,), (1,)), ((), ()))
_TRANS_A = (((0,), (0,)), ((), ()))


def _cparams(sem, vmem_mb):
    return pltpu.CompilerParams(dimension_semantics=sem, vmem_limit_bytes=vmem_mb * 1024 * 1024)


def _tile(dim, pref):
    t = min(dim, pref)
    assert dim % t == 0, (dim, pref)
    return t


def _matmul_kernel(a_ref, w_ref, o_ref):
    o_ref[...] = jnp.dot(a_ref[...], w_ref[...], preferred_element_type=F32).astype(o_ref.dtype)


def matmul(a, w, layer, out_dtype, tm=1024, tn=1024):
    m, k = a.shape
    _, k2, n = w.shape
    assert k == k2
    tm, tn = _tile(m, tm), _tile(n, tn)
    return pl.pallas_call(
        _matmul_kernel,
        grid=(m // tm, n // tn),
        in_specs=[pl.BlockSpec((tm, k), lambda i, j: (i, 0)), pl.BlockSpec((None, k, tn), lambda i, j: (layer, 0, j))],
        out_specs=pl.BlockSpec((tm, tn), lambda i, j: (i, j)),
        out_shape=jax.ShapeDtypeStruct((m, n), out_dtype),
        compiler_params=_cparams(("parallel", "parallel"), 56),
        name="matmul",
    )(a, w)


def _layer_norm(y, g, b):
    mu = jnp.mean(y, axis=-1, keepdims=True)
    yc = y - mu
    var = jnp.mean(yc * yc, axis=-1, keepdims=True)
    return yc * lax.rsqrt(var + LN_EPS) * g + b


def _round_bf16_bits(v):
    return lax.bitcast_convert_type(v.astype(BF16).astype(F32), U32)


def _pack_rows(yn, o_ref, rows):
    half = yn.shape[1] // 2
    packed = (_round_bf16_bits(yn[:, :half]) >> 16) | (_round_bf16_bits(yn[:, half:]) & jnp.uint32(0xFFFF0000))
    for s in range(ROW_SLAB):
        o_ref[pl.ds(s, rows, stride=ROW_SLAB), :] = packed[:, s * LANES:(s + 1) * LANES]


def _unpack_rows(ref, base, rows, dtype):
    lo, hi = [], []
    for s in range(ROW_SLAB):
        u = ref[pl.ds(base + s, rows, stride=ROW_SLAB), :]
        lo.append(lax.bitcast_convert_type(u << 16, F32).astype(dtype))
        hi.append(lax.bitcast_convert_type(u & jnp.uint32(0xFFFF0000), F32).astype(dtype))
    return jnp.concatenate(lo, axis=1), jnp.concatenate(hi, axis=1)


def _res_ln_proj_kernel(x_ref, h_ref, g_ref, b_ref, wq_ref, of_ref, oq_ref, *, alpha):
    yn = _layer_norm(alpha * x_ref[...] + h_ref[...].astype(F32), g_ref[...], b_ref[...])
    of_ref[...] = yn
    oq_ref[...] = jnp.dot(yn.astype(BF16), wq_ref[...], preferred_element_type=F32).astype(oq_ref.dtype)


def residual_layer_norm_proj(x, h, g, b, wq, layer, alpha, tm=256):
    t, d = x.shape
    nq = wq.shape[2]
    tm = _tile(t, tm)
    row = pl.BlockSpec((tm, d), lambda i: (i, 0))
    vec = pl.BlockSpec((1, d), lambda i: (0, 0))
    return pl.pallas_call(
        functools.partial(_res_ln_proj_kernel, alpha=alpha),
        grid=(t // tm,),
        in_specs=[row, row, vec, vec, pl.BlockSpec((None, d, nq), lambda i: (layer, 0, 0))],
        out_specs=[row, pl.BlockSpec((tm, nq), lambda i: (i, 0))],
        out_shape=[jax.ShapeDtypeStruct((t, d), F32), jax.ShapeDtypeStruct((t, nq), BF16)],
        compiler_params=_cparams(("parallel",), 56),
        name="residual_layer_norm_proj",
    )(x, h, g.reshape(1, d), b.reshape(1, d), wq)


def _memory_attention(q_ref, kv_ref):
    width = q_ref.shape[1]
    head_dim = width // MEM_HEADS
    scale = float(head_dim) ** -0.5
    outs = []
    for h in range(MEM_HEADS):
        cols = slice(h * head_dim, (h + 1) * head_dim)
        sc = lax.dot_general(q_ref[:, cols], kv_ref[:, cols], _TRANS_B, preferred_element_type=F32) * scale
        p = jnp.exp(sc - jnp.max(sc, axis=-1, keepdims=True))
        den = jnp.sum(p, axis=-1, keepdims=True)
        vh = kv_ref[:, width + h * head_dim:width + (h + 1) * head_dim]
        outs.append((jnp.dot(p.astype(BF16), vh, preferred_element_type=F32) / den).astype(BF16))
    return jnp.concatenate(outs, axis=1)


def _mem_attn_proj_kernel(q_ref, kv_ref, wo_ref, h_ref):
    h_ref[...] = jnp.dot(_memory_attention(q_ref, kv_ref), wo_ref[...], preferred_element_type=F32).astype(h_ref.dtype)


def memory_attention_proj(q, kv, wo, layer, tm=512):
    t, width = q.shape
    bsz, m, w2 = kv.shape
    d = wo.shape[2]
    tm = _tile(t // bsz, tm)
    tiles_per_batch = t // bsz // tm
    assert w2 == 2 * width
    return pl.pallas_call(
        _mem_attn_proj_kernel,
        grid=(t // tm,),
        in_specs=[pl.BlockSpec((tm, width), lambda i: (i, 0)),
                  pl.BlockSpec((None, m, w2), lambda i: (i // tiles_per_batch, 0, 0)),
                  pl.BlockSpec((None, width, d), lambda i: (layer, 0, 0))],
        out_specs=pl.BlockSpec((tm, d), lambda i: (i, 0)),
        out_shape=jax.ShapeDtypeStruct((t, d), BF16),
        compiler_params=_cparams(("parallel",), 48),
        name="memory_attention_proj",
    )(q, kv, wo)


def _res_ln_route_kernel(x_ref, h_ref, g_ref, b_ref, wr_ref, of_ref, og_ref, lg_ref, *, alpha):
    yn = _layer_norm(alpha * x_ref[...] + h_ref[...].astype(F32), g_ref[...], b_ref[...])
    of_ref[...] = yn
    _pack_rows(yn, og_ref, yn.shape[0])
    y_hi = yn.astype(BF16)
    y_lo = (yn - y_hi.astype(F32)).astype(BF16)
    both = jnp.dot(y_hi, wr_ref[...], preferred_element_type=F32)
    lg_ref[...] = both[:, :LANES] + both[:, LANES:] + jnp.dot(y_lo, wr_ref[:, :LANES], preferred_element_type=F32)


def residual_layer_norm_route(x, h, g, b, router_w2, alpha, tm=256):
    t, d = x.shape
    tm = _tile(t, tm)
    assert d == 2 * ROW_SLAB * LANES
    row = pl.BlockSpec((tm, d), lambda i: (i, 0))
    vec = pl.BlockSpec((1, d), lambda i: (0, 0))
    return pl.pallas_call(
        functools.partial(_res_ln_route_kernel, alpha=alpha),
        grid=(t // tm,),
        in_specs=[row, row, vec, vec, pl.BlockSpec((d, 2 * LANES), lambda i: (0, 0))],
        out_specs=[row, pl.BlockSpec((tm * ROW_SLAB, LANES), lambda i: (i, 0)), pl.BlockSpec((tm, LANES), lambda i: (i, 0))],
        out_shape=[jax.ShapeDtypeStruct((t, d), F32), jax.ShapeDtypeStruct((t * ROW_SLAB, LANES), U32),
                   jax.ShapeDtypeStruct((t, LANES), F32)],
        compiler_params=_cparams(("parallel",), 48),
        name="residual_layer_norm_route",
    )(x, h, g.reshape(1, d), b.reshape(1, d), router_w2)


def _gla_kernel(q_ref, k_ref, v_ref, g_ref, gd_ref, wu_ref, bg_ref, ng_ref, o_ref, st_ref, *, n_groups, q_scale):
    rows_g = GLA_GROUP * CHUNK

    @pl.when(pl.program_id(2) == 0)
    def _():
        st_ref[...] = jnp.zeros_like(st_ref)

    ri = lax.broadcasted_iota(I32, (rows_g, rows_g), 0)
    ci = lax.broadcasted_iota(I32, (rows_g, rows_g), 1)
    chunk_gap = ri // CHUNK - ci // CHUNK
    causal = (chunk_gap == 0) & (ri >= ci)
    tri = causal.astype(BF16)

    def stack(rows_of_chunk):
        return jnp.concatenate([jnp.broadcast_to(r, (CHUNK, r.shape[1])) for r in rows_of_chunk], axis=0)

    def group(gi, carry):
        rows = pl.ds(pl.multiple_of(gi * rows_g, rows_g), rows_g)
        q = q_ref[rows, :].astype(F32) * q_scale
        k = k_ref[rows, :].astype(F32)
        v = v_ref[rows, :]
        gate = jnp.dot(gd_ref[rows, :].astype(BF16), wu_ref[...], preferred_element_type=F32) + bg_ref[...]
        la = (jnp.minimum(gate, 0.0) - jnp.log(1.0 + jnp.exp(-jnp.abs(gate)))) * (1.0 / GLA_GATE_TAU)
        hi = la.astype(BF16)
        r1 = la - hi.astype(F32)
        mid = r1.astype(BF16)
        lo = (r1 - mid.astype(F32)).astype(BF16)
        bc = (jnp.dot(tri, hi, preferred_element_type=F32) + jnp.dot(tri, mid, preferred_element_type=F32)
              + jnp.dot(tri, lo, preferred_element_type=F32))
        bl = [bc[(c + 1) * CHUNK - 1:(c + 1) * CHUNK, :] for c in range(GLA_GROUP)]
        zero = jnp.zeros_like(bl[0])
        before = [sum(bl[:c], zero) for c in range(GLA_GROUP)]
        after = [sum(bl[c + 1:], zero) for c in range(GLA_GROUP)]
        q_e = q * jnp.exp(bc)
        k_e = k * jnp.exp(-bc)
        k_d = k_e * stack([jnp.exp(r) for r in bl])
        k_eb = k_e.astype(BF16)
        k_db = k_d.astype(BF16)
        att = jnp.where(causal, lax.dot_general(q_e.astype(BF16), k_eb, _TRANS_B, preferred_element_type=F32), 0.0)
        for gap in range(1, GLA_GROUP):
            between = [jnp.exp(sum(bl[c - gap + 1:c], zero)) if c >= gap else zero for c in range(GLA_GROUP)]
            q_gap = q_e if gap == 1 else q_e * stack(between)
            att = att + jnp.where(chunk_gap == gap,
                                  lax.dot_general(q_gap.astype(BF16), k_db, _TRANS_B, preferred_element_type=F32), 0.0)
        st = st_ref[...]
        q_st = (q_e * stack([jnp.exp(r) for r in before])).astype(BF16)
        o = jnp.dot(att.astype(BF16), v, preferred_element_type=F32) + lax.dot_general(
            q_st, st.astype(BF16), _TRANS_B, preferred_element_type=F32)
        k_up = (k_d * stack([jnp.exp(r) for r in after])).astype(BF16)
        st_ref[...] = st * jnp.exp(before[-1] + bl[-1]) + lax.dot_general(v, k_up, _TRANS_A, preferred_element_type=F32)
        on = o * lax.rsqrt(jnp.mean(o * o, axis=-1, keepdims=True) + LN_EPS) * ng_ref[...]
        g = g_ref[rows, :].astype(F32)
        o_ref[rows, :] = (on * (g / (1.0 + jnp.exp(-g)))).astype(o_ref.dtype)
        return carry

    lax.fori_loop(0, n_groups, group, 0, unroll=2)


def gla_core(z, gd, wu, bg, ng, layer, rows_per_step=512):
    b, s, zc = z.shape
    hk = wu.shape[3]
    hv = ng.shape[3]
    assert zc == GLA_HEADS * (2 * hk + 2 * hv) and hv == 2 * hk
    rb = _tile(s, rows_per_step)
    assert rb % (2 * GLA_GROUP * CHUNK) == 0
    kq = GLA_HEADS
    kv = 2 * GLA_HEADS * hk // hv
    kg = kv + GLA_HEADS
    return pl.pallas_call(
        functools.partial(_gla_kernel, n_groups=rb // (GLA_GROUP * CHUNK), q_scale=float(hk) ** -0.5),
        grid=(b, GLA_HEADS, s // rb),
        in_specs=[
            pl.BlockSpec((None, rb, hk), lambda bi, h, r: (bi, r, h)),
            pl.BlockSpec((None, rb, hk), lambda bi, h, r: (bi, r, kq + h)),
            pl.BlockSpec((None, rb, hv), lambda bi, h, r: (bi, r, kv + h)),
            pl.BlockSpec((None, rb, hv), lambda bi, h, r: (bi, r, kg + h)),
            pl.BlockSpec((None, rb, LANES), lambda bi, h, r: (bi, r, 0)),
            pl.BlockSpec((None, None, LANES, hk), lambda bi, h, r: (layer, h, 0, 0)),
            pl.BlockSpec((None, None, 1, hk), lambda bi, h, r: (layer, h, 0, 0)),
            pl.BlockSpec((None, None, 1, hv), lambda bi, h, r: (layer, h, 0, 0)),
        ],
        out_specs=pl.BlockSpec((None, rb, hv), lambda bi, h, r: (bi, r, h)),
        out_shape=jax.ShapeDtypeStruct((b, s, GLA_HEADS * hv), BF16),
        scratch_shapes=[pltpu.VMEM((hv, hk), F32)],
        compiler_params=_cparams(("parallel", "parallel", "arbitrary"), 40),
        name="gla_core",
    )(z, z, z, z, gd, wu, bg, ng)


def _band_attn_kernel(q_ref, k_ref, v_ref, bias_ref, o_ref, kp_ref, vp_ref, *, n_blocks, scale):
    pad = ATT_LEFT_CHUNKS * CHUNK
    s = n_blocks * ATT_QB
    kp_ref[pl.ds(0, pad), :] = jnp.zeros((pad, ATT_HEAD_DIM), kp_ref.dtype)
    vp_ref[pl.ds(0, pad), :] = jnp.zeros((pad, ATT_HEAD_DIM), vp_ref.dtype)
    kp_ref[pl.ds(pad, s), :] = k_ref[...]
    vp_ref[pl.ds(pad, s), :] = v_ref[...]
    last_tile = bias_ref.shape[0] - 1

    def block(i, carry):
        start = pl.multiple_of(i * ATT_QB, ATT_QB)
        q = (q_ref[pl.ds(start, ATT_QB), :].astype(F32) * scale).astype(BF16)
        kb = kp_ref[pl.ds(start, ATT_WIN), :]
        vb = vp_ref[pl.ds(start, ATT_WIN), :]
        sc = lax.dot_general(q, kb, _TRANS_B, preferred_element_type=F32) + bias_ref[jnp.minimum(i, last_tile)]
        p = jnp.exp(sc - jnp.max(sc, axis=-1, keepdims=True))
        den = jnp.sum(p, axis=-1, keepdims=True)
        o = jnp.dot(p.astype(BF16), vb, preferred_element_type=F32) / den
        o_ref[pl.ds(start, ATT_QB), :] = o.astype(o_ref.dtype)
        return carry

    lax.fori_loop(0, n_blocks, block, 0, unroll=ATT_UNROLL)


def band_attention_core(qkv, bias):
    b, s, c3 = qkv.shape
    h, n_tiles = bias.shape[:2]
    assert c3 == 3 * h * ATT_HEAD_DIM and s % (ATT_UNROLL * ATT_QB) == 0
    blk = lambda off: pl.BlockSpec((None, s, ATT_HEAD_DIM), lambda bi, hi: (bi, 0, off + hi))
    return pl.pallas_call(
        functools.partial(_band_attn_kernel, n_blocks=s // ATT_QB, scale=float(ATT_HEAD_DIM) ** -0.5),
        grid=(b, h),
        in_specs=[blk(0), blk(h), blk(2 * h),
                  pl.BlockSpec((None, n_tiles, ATT_QB, ATT_WIN), lambda bi, hi: (hi, 0, 0, 0))],
        out_specs=blk(0),
        out_shape=jax.ShapeDtypeStruct((b, s, h * ATT_HEAD_DIM), BF16),
        scratch_shapes=[pltpu.VMEM((s + ATT_LEFT_CHUNKS * CHUNK, ATT_HEAD_DIM), BF16)] * 2,
        compiler_params=_cparams(("parallel", "parallel"), 40),
        name="band_attention_core",
    )(qkv, qkv, qkv, bias)


def _rel_bias(rel_table):
    h = rel_table.shape[0]
    pad = ATT_LEFT_CHUNKS * CHUNK
    assert ATT_REL_CLIP >= CHUNK and 2 * ATT_REL_CLIP + 1 >= ATT_BAND - ATT_REL_CLIP
    rev = rel_table[:, ::-1].astype(F32)
    lead = pad - ATT_REL_CLIP + CHUNK - 1
    ext = jnp.concatenate([jnp.broadcast_to(rev[:, :1], (h, lead)), rev[:, :ATT_BAND - ATT_REL_CLIP]], axis=1)
    band = jnp.stack([ext[:, CHUNK - 1 - i:CHUNK - 1 - i + ATT_BAND] for i in range(CHUNK)], axis=1)
    neg = jnp.full((h, CHUNK, ATT_WIN), -jnp.inf, F32)
    window = jnp.concatenate(
        [lax.dynamic_update_slice(neg, band, (0, 0, c * CHUNK)) for c in range(ATT_QUERY_CHUNKS)], axis=1)
    n_tiles = -(-pad // ATT_QB) + 1
    slot = np.arange(ATT_WIN)[None, :]
    keep = np.stack([np.broadcast_to(slot >= pad - n * ATT_QB, (ATT_QB, ATT_WIN)) for n in range(n_tiles)])
    return jnp.where(keep[None], window[:, None], -jnp.inf)


def _top2_of4(a, b, c, d):
    hi1, lo1 = jnp.maximum(a, b), jnp.minimum(a, b)
    hi2, lo2 = jnp.maximum(c, d), jnp.minimum(c, d)
    return jnp.maximum(hi1, hi2) + jnp.maximum(jnp.minimum(hi1, hi2), jnp.maximum(lo1, lo2))


def _argmax_first(vals):
    best, idx = vals[0], jnp.zeros(vals[0].shape, I32)
    for j in range(1, len(vals)):
        better = vals[j] > best
        best = jnp.where(better, vals[j], best)
        idx = jnp.where(better, j, idx)
    return best, idx


def _router_kernel(lg_ref, br_ref, route_ref, wts_ref, cnt_ref, upper_ref, *, n_experts):
    tr = lg_ref.shape[0]
    per_group = n_experts // MOE_GROUPS

    @pl.when(pl.program_id(0) == 0)
    def _():
        cnt_ref[...] = jnp.zeros_like(cnt_ref)
        earlier = lax.broadcasted_iota(I32, (tr, tr), 0) < lax.broadcasted_iota(I32, (tr, tr), 1)
        upper_ref[...] = earlier.astype(BF16)

    lt = lg_ref[...].T[:n_experts, :] + br_ref[:n_experts, :]
    ex = jnp.exp(lt - jnp.max(lt, axis=0, keepdims=True))
    probs = ex / jnp.sum(ex, axis=0, keepdims=True)
    p = [probs[e:e + 1, :] for e in range(n_experts)]

    assert per_group == 4
    scores = [_top2_of4(*p[per_group * g:per_group * (g + 1)]) for g in range(MOE_GROUPS)]
    _, best_group = _argmax_first(scores)
    in_group = [sum(jnp.where(best_group == g, p[per_group * g + j], 0.0) for g in range(MOE_GROUPS))
                for j in range(per_group)]
    p1, j1 = _argmax_first(in_group)
    p2, j2 = _argmax_first([jnp.where(j1 == j, -1.0, in_group[j]) for j in range(per_group)])
    e1 = best_group * per_group + j1
    e2 = best_group * per_group + j2
    w1 = p1 / (p1 + p2)
    w2 = p2 / (p1 + p2)

    erow = lax.broadcasted_iota(I32, (n_experts, tr), 0)
    hit1 = erow == e1
    hit2 = erow == e2
    onehot = hit1.astype(F32) + hit2.astype(F32)
    before = cnt_ref[:, 0:1] + jnp.dot(onehot.astype(BF16), upper_ref[...], preferred_element_type=F32)
    rank1 = jnp.sum(jnp.where(hit1, before, 0.0), axis=0, keepdims=True).astype(I32)
    rank2 = jnp.sum(jnp.where(hit2, before, 0.0), axis=0, keepdims=True).astype(I32)
    cnt_ref[...] += jnp.sum(onehot, axis=1, keepdims=True)

    rrow = lax.broadcasted_iota(I32, (SUBLANES, tr), 0)
    route_ref[...] = jnp.where(rrow == 0, e1, jnp.where(rrow == 1, e2, jnp.where(rrow == 2, rank1, rank2)))
    wrow = lax.broadcasted_iota(I32, (LANES, tr), 0)
    wts_ref[...] = jnp.where(wrow == 0, w1, jnp.where(wrow == 1, w2, 0.0)).T


def moe_router(logits, br, n_experts, tr=1024):
    t = logits.shape[0]
    tr = _tile(t, tr)
    return pl.pallas_call(
        functools.partial(_router_kernel, n_experts=n_experts),
        grid=(t // tr,),
        in_specs=[pl.BlockSpec((tr, LANES), lambda i: (i, 0)), pl.BlockSpec((LANES, 1), lambda i: (0, 0))],
        out_specs=[
            pl.BlockSpec((SUBLANES, tr), lambda i: (0, i)),
            pl.BlockSpec((tr, LANES), lambda i: (i, 0)),
            pl.BlockSpec((n_experts, LANES), lambda i: (0, 0)),
        ],
        out_shape=[
            jax.ShapeDtypeStruct((SUBLANES, t), I32),
            jax.ShapeDtypeStruct((t, LANES), F32),
            jax.ShapeDtypeStruct((n_experts, LANES), F32),
        ],
        scratch_shapes=[pltpu.VMEM((tr, tr), BF16)],
        compiler_params=_cparams(("arbitrary",), 32),
        name="moe_router",
    )(logits, br)


def _row_copy(src_ref, src_row, dst_ref, dst_row, sem):
    return pltpu.make_async_copy(
        src_ref.at[pl.ds(pl.multiple_of(src_row * ROW_SLAB, ROW_SLAB), ROW_SLAB), :],
        dst_ref.at[pl.ds(pl.multiple_of(dst_row * ROW_SLAB, ROW_SLAB), ROW_SLAB), :],
        sem)


def _rows_wait(src_ref, dst_ref, n_rows, sem):
    span = pl.ds(0, n_rows * ROW_SLAB)
    pltpu.make_async_copy(src_ref.at[span, :], dst_ref.at[span, :], sem).wait()


def _dispatch_kernel(pad_start_ref, pad_count_ref, pos_ref, xg_ref, xs_ref, zero_ref, sem, pad_sem, *, td, n_pads):
    @pl.when(pl.program_id(0) == 0)
    def _():
        zero_ref[...] = jnp.zeros_like(zero_ref)
        for s in range(n_pads):
            def fill(r, carry, s=s):
                _row_copy(zero_ref, 0, xs_ref, pad_start_ref[s] + r, pad_sem).start()
                return carry

            lax.fori_loop(0, pad_count_ref[s], fill, 0)
        for s in range(n_pads):
            def drain(r, carry):
                _row_copy(zero_ref, 0, xs_ref, 0, pad_sem).wait()
                return carry

            lax.fori_loop(0, pad_count_ref[s], drain, 0)

    def issue(t, carry):
        for slot in range(2):
            _row_copy(xg_ref, t, xs_ref, pos_ref[0, 0, slot * td + t], sem).start()
        return carry

    lax.fori_loop(0, td, issue, 0, unroll=4)
    for _ in range(2):
        _rows_wait(xg_ref, xs_ref, td, sem)


def moe_dispatch(xg, pos_blocks, pad_start, pad_count, n_sorted_rows, td):
    t = xg.shape[0] // ROW_SLAB
    return pl.pallas_call(
        functools.partial(_dispatch_kernel, td=td, n_pads=pad_start.shape[0]),
        grid_spec=pltpu.PrefetchScalarGridSpec(
            num_scalar_prefetch=2,
            grid=(t // td,),
            in_specs=[
                pl.BlockSpec((1, 1, 2 * td), lambda i, ps, pc: (i, 0, 0), memory_space=pltpu.SMEM),
                pl.BlockSpec((td * ROW_SLAB, LANES), lambda i, ps, pc: (i, 0)),
            ],
            out_specs=pl.BlockSpec(memory_space=pl.ANY),
            scratch_shapes=[pltpu.VMEM((ROW_SLAB, LANES), U32), pltpu.SemaphoreType.DMA, pltpu.SemaphoreType.DMA],
        ),
        out_shape=jax.ShapeDtypeStruct((n_sorted_rows * ROW_SLAB, LANES), U32),
        compiler_params=_cparams(("arbitrary",), 32),
        name="moe_dispatch",
    )(pad_start, pad_count, pos_blocks, xg)


def _expert_kernel(te_ref, nu_ref, first_ref, next_ref, xs_ref, wgu_hbm, wd_hbm, y_ref,
                   gu_f32_ref, dn_f32_ref, wgu_ref, wd_ref, sem, *, tm, layer):
    i = pl.program_id(0)
    used = i < nu_ref[0]

    def weight_copies(e):
        return (pltpu.make_async_copy(wgu_hbm.at[layer, e], gu_f32_ref, sem.at[0]),
                pltpu.make_async_copy(wd_hbm.at[layer, e], dn_f32_ref, sem.at[1]))

    @pl.when(i == 0)
    def _():
        for c in weight_copies(te_ref[0]):
            c.start()

    @pl.when(used & (first_ref[i] == 1))
    def _():
        for c in weight_copies(te_ref[i]):
            c.wait()
        wgu_ref[...] = gu_f32_ref[...].astype(BF16)
        wd_ref[...] = dn_f32_ref[...].astype(BF16)

        @pl.when(next_ref[i] >= 0)
        def _():
            for c in weight_copies(next_ref[i]):
                c.start()

    @pl.when(used)
    def _():
        half = wgu_ref.shape[0] // 2
        ff = wd_ref.shape[0]
        x_lo, x_hi = _unpack_rows(xs_ref, 0, tm, BF16)
        gu = (jnp.dot(x_lo, wgu_ref[:half, :], preferred_element_type=F32)
              + jnp.dot(x_hi, wgu_ref[half:, :], preferred_element_type=F32))
        gate, up = gu[:, :ff], gu[:, ff:]
        hid = (gate / (1.0 + jnp.exp(-gate)) * up).astype(BF16)
        _pack_rows(jnp.dot(hid, wd_ref[...], preferred_element_type=F32), y_ref, tm)

    @pl.when(jnp.logical_not(used))
    def _():
        y_ref[...] = jnp.zeros_like(y_ref)


def moe_experts(xs, wgu, wd, layer, tile_expert, n_used, seg_first, seg_next, tm):
    n_tiles = xs.shape[0] // (tm * ROW_SLAB)
    _, _, d, ff2 = wgu.shape
    rows = pl.BlockSpec((tm * ROW_SLAB, LANES), lambda i, *_: (i, 0))
    return pl.pallas_call(
        functools.partial(_expert_kernel, tm=tm, layer=layer),
        grid_spec=pltpu.PrefetchScalarGridSpec(
            num_scalar_prefetch=4,
            grid=(n_tiles,),
            in_specs=[rows, pl.BlockSpec(memory_space=pl.ANY), pl.BlockSpec(memory_space=pl.ANY)],
            out_specs=rows,
            scratch_shapes=[pltpu.VMEM((d, ff2), F32), pltpu.VMEM((ff2 // 2, d), F32),
                            pltpu.VMEM((d, ff2), BF16), pltpu.VMEM((ff2 // 2, d), BF16),
                            pltpu.SemaphoreType.DMA((2,))],
        ),
        out_shape=jax.ShapeDtypeStruct(xs.shape, U32),
        compiler_params=_cparams(("arbitrary",), 56),
        name="moe_experts",
    )(tile_expert, n_used, seg_first, seg_next, xs, wgu, wd)


def _combine_ln_kernel(pos_ref, pos_next_ref, wts_ref, x_ref, ys_ref, g_ref, b_ref, of_ref, ob_ref, buf_ref, sem,
                       *, tc, alpha):
    i = pl.program_id(0)
    cur = i % 2

    def start_gather(p_ref, b):
        def issue(t, carry):
            for slot in range(2):
                _row_copy(ys_ref, p_ref[0, 0, slot * tc + t], buf_ref.at[b], slot * tc + t, sem.at[b]).start()
            return carry

        lax.fori_loop(0, tc, issue, 0, unroll=4)

    @pl.when(i == 0)
    def _():
        start_gather(pos_ref, 0)

    @pl.when(i + 1 < pl.num_programs(0))
    def _():
        start_gather(pos_next_ref, 1 - cur)

    rows_in = buf_ref.at[cur]
    _rows_wait(ys_ref, rows_in, 2 * tc, sem.at[cur])

    def group(gi, carry):
        r0 = pl.multiple_of(gi * COMBINE_ROWS, COMBINE_ROWS)
        rows = pl.ds(r0, COMBINE_ROWS)
        y = alpha * x_ref[rows, :]
        for slot in range(2):
            lo, hi = _unpack_rows(rows_in, (slot * tc + r0) * ROW_SLAB, COMBINE_ROWS, F32)
            y = y + wts_ref[rows, slot:slot + 1] * jnp.concatenate([lo, hi], axis=1)
        yn = _layer_norm(y, g_ref[...], b_ref[...])
        of_ref[rows, :] = yn
        ob_ref[rows, :] = yn.astype(BF16)
        return carry

    lax.fori_loop(0, tc // COMBINE_ROWS, group, 0, unroll=2)


def moe_combine_layer_norm(x, ys, pos_blocks, wts, g, b, alpha, tc):
    t, d = x.shape
    n = t // tc
    row = pl.BlockSpec((tc, d), lambda i: (i, 0))
    vec = pl.BlockSpec((1, d), lambda i: (0, 0))
    return pl.pallas_call(
        functools.partial(_combine_ln_kernel, tc=tc, alpha=alpha),
        grid=(n,),
        in_specs=[
            pl.BlockSpec((1, 1, 2 * tc), lambda i: (i, 0, 0), memory_space=pltpu.SMEM),
            pl.BlockSpec((1, 1, 2 * tc), lambda i: (jnp.minimum(i + 1, n - 1), 0, 0), memory_space=pltpu.SMEM),
            pl.BlockSpec((tc, LANES), lambda i: (i, 0)),
            row,
            pl.BlockSpec(memory_space=pl.ANY),
            vec,
            vec,
        ],
        out_specs=[row, row],
        out_shape=[jax.ShapeDtypeStruct((t, d), F32), jax.ShapeDtypeStruct((t, d), BF16)],
        scratch_shapes=[pltpu.VMEM((2, 2 * tc * ROW_SLAB, LANES), U32), pltpu.SemaphoreType.DMA((2,))],
        compiler_params=_cparams(("arbitrary",), 48),
        name="moe_combine_layer_norm",
    )(pos_blocks, pos_blocks, wts, x, ys, g.reshape(1, d), b.reshape(1, d))


def _pos_blocks(pos, tile):
    t = pos.shape[1]
    return pos.reshape(2, t // tile, tile).transpose(1, 0, 2).reshape(t // tile, 1, 2 * tile)


def grouped_moe_layer_norm(xf, xg, logits, br, wgu, wd, layer, g, b, alpha, tm=256, td=512, tc=256):
    t, d = xf.shape
    n_experts = wgu.shape[1]
    route, wts, counts = moe_router(logits, br, n_experts)
    cnt = counts[:, 0].astype(I32)
    padded = (cnt + tm - 1) // tm * tm
    ends = jnp.cumsum(padded)
    offs = ends - padded
    expert_ids = jnp.arange(n_experts, dtype=I32)
    pos = jnp.sum(jnp.where(route[0:2, :, None] == expert_ids, offs, 0), axis=-1) + route[2:4]
    n_tiles = (2 * t) // tm + n_experts
    tile_starts = jnp.arange(n_tiles, dtype=I32) * tm
    tile_expert = jnp.minimum(jnp.sum((ends[None, :] <= tile_starts[:, None]).astype(I32), axis=1), n_experts - 1)
    n_used = (ends[-1:] // tm).astype(I32)
    tile_ids = jnp.arange(n_tiles, dtype=I32)
    of_tile = lambda table: jnp.sum(jnp.where(tile_expert[:, None] == expert_ids, table, 0), axis=1)
    seg_first = ((tile_ids == of_tile(offs // tm)) & (tile_ids < n_used)).astype(I32)
    next_start = of_tile(ends // tm)
    expert_at = jnp.sum(jnp.where(next_start[:, None] == tile_ids, tile_expert, 0), axis=1)
    seg_next = jnp.where(next_start < n_used, expert_at, -1).astype(I32)
    pad_start = jnp.concatenate([offs + cnt, ends[-1:]])
    pad_count = jnp.concatenate([padded - cnt, n_tiles * tm - ends[-1:]])
    xs = moe_dispatch(xg, _pos_blocks(pos, td), pad_start, pad_count, n_tiles * tm, td)
    ys = moe_experts(xs, wgu, wd, layer, tile_expert, n_used, seg_first, seg_next, tm)
    return moe_combine_layer_norm(xf, ys, _pos_blocks(pos, tc), wts, g, b, alpha, tc)


def kernel(x, mem, gla_w_in, gla_w_gate_down, gla_w_gate_up, gla_b_gate, gla_norm_g, gla_w_o, att_w_qkv, att_rel_bias, att_w_o, mx_w_q, mx_w_kv, mx_w_o, router_w, router_b, moe_w_gu, moe_w_down, ln_g, ln_b):
    bsz, seq, d = x.shape
    t = bsz * seq
    depth = ln_g.shape[0]
    alpha = (2.0 * depth) ** 0.25
    n_experts = router_w.shape[1]
    n_gla = gla_w_in.shape[0]
    hk = gla_w_gate_up.shape[2] // GLA_HEADS
    hv = gla_norm_g.shape[1] // GLA_HEADS
    rank = gla_w_gate_down.shape[2]
    mem_len = mem.shape[1]

    xf = x.reshape(t, d)
    xb = xf.astype(BF16)
    mem_b = mem.reshape(bsz * mem_len, d).astype(BF16)
    wr = jnp.pad(router_w, ((0, 0), (0, LANES - n_experts)))
    wr_hi = wr.astype(BF16)
    wr2 = jnp.concatenate([wr_hi, (wr - wr_hi.astype(F32)).astype(BF16)], axis=1)
    br = jnp.pad(router_b, (0, LANES - n_experts)).reshape(LANES, 1)
    w_in, w_go, w_qkv, w_ao = (w.astype(BF16) for w in (gla_w_in, gla_w_o, att_w_qkv, att_w_o))
    w_mq, w_mkv, w_mo = (w.astype(BF16) for w in (mx_w_q, mx_w_kv, mx_w_o))
    w_gd = jnp.pad(gla_w_gate_down, ((0, 0), (0, 0), (0, LANES - rank))).astype(BF16)
    w_gup = jnp.pad(gla_w_gate_up.reshape(n_gla, rank, GLA_HEADS, hk).transpose(0, 2, 1, 3),
                    ((0, 0), (0, 0), (0, LANES - rank), (0, 0))).astype(BF16)
    b_gate = gla_b_gate.reshape(n_gla, GLA_HEADS, 1, hk)
    n_gain = gla_norm_g.reshape(n_gla, GLA_HEADS, 1, hv)

    for i in range(depth):
        j = i // 2
        if i % 2 == 0:
            z = matmul(xb, w_in, j, BF16)
            gd = matmul(xb, w_gd, j, F32)
            o = gla_core(z.reshape(bsz, seq, -1), gd.reshape(bsz, seq, LANES), w_gup, b_gate, n_gain, j)
            h = matmul(o.reshape(t, -1), w_go, j, BF16)
        else:
            qkv = matmul(xb, w_qkv, j, BF16)
            o = band_attention_core(qkv.reshape(bsz, seq, -1), _rel_bias(att_rel_bias[j]))
            h = matmul(o.reshape(t, -1), w_ao, j, BF16)
        xf, q = residual_layer_norm_proj(xf, h, ln_g[i, 0], ln_b[i, 0], w_mq, i, alpha)

        kv = matmul(mem_b, w_mkv, i, BF16)
        h = memory_attention_proj(q, kv.reshape(bsz, mem_len, -1), w_mo, i)
        xf, xg, logits = residual_layer_norm_route(xf, h, ln_g[i, 1], ln_b[i, 1], wr2, alpha)

        xf, xb = grouped_moe_layer_norm(xf, xg, logits, br, moe_w_gu, moe_w_down, i, ln_g[i, 2], ln_b[i, 2], alpha)
    return xf.reshape(bsz, seq, d)
```

```python
import functools

import jax
import jax.numpy as jnp
import numpy as np
from jax import lax
from jax.experimental import pallas as pl
from jax.experimental.pallas import tpu as pltpu

F32 = jnp.float32
BF16 = jnp.bfloat16
U32 = jnp.uint32
I32 = jnp.int32

CHUNK = 64
LN_EPS = 1e-5
GLA_HEADS = 4
GLA_GATE_TAU = 16.0
GLA_GROUP = 4
ATT_HEAD_DIM = 128
ATT_LEFT_CHUNKS = 8
ATT_BAND = (ATT_LEFT_CHUNKS + 1) * CHUNK
ATT_REL_CLIP = 256
ATT_QUERY_CHUNKS = 4
ATT_QB = ATT_QUERY_CHUNKS * CHUNK
ATT_WIN = (ATT_LEFT_CHUNKS + ATT_QUERY_CHUNKS) * CHUNK
ATT_UNROLL = 8
MEM_HEADS = 4
MOE_GROUPS = 4

LANES = 128
SUBLANES = 8

ROW_SLAB = 16
COMBINE_ROWS = 16

_TRANS_B = (((1,), (1,)), ((), ()))
_TRANS_A = (((0,), (0,)), ((), ()))


def _cparams(sem, vmem_mb):
    return pltpu.CompilerParams(dimension_semantics=sem, vmem_limit_bytes=vmem_mb * 1024 * 1024)


def _tile(dim, pref):
    t = min(dim, pref)
    assert dim % t == 0, (dim, pref)
    return t


def _matmul_kernel(a_ref, w_ref, o_ref):
    o_ref[...] = jnp.dot(a_ref[...], w_ref[...], preferred_element_type=F32).astype(o_ref.dtype)


def matmul(a, w, layer, out_dtype, tm=1024, tn=1024):
    m, k = a.shape
    _, k2, n = w.shape
    assert k == k2
    tm, tn = _tile(m, tm), _tile(n, tn)
    return pl.pallas_call(
        _matmul_kernel,
        grid=(m // tm, n // tn),
        in_specs=[pl.BlockSpec((tm, k), lambda i, j: (i, 0)), pl.BlockSpec((None, k, tn), lambda i, j: (layer, 0, j))],
        out_specs=pl.BlockSpec((tm, tn), lambda i, j: (i, j)),
        out_shape=jax.ShapeDtypeStruct((m, n), out_dtype),
        compiler_params=_cparams(("parallel", "parallel"), 56),
        name="matmul",
    )(a, w)


def _matmul_f32w_kernel(a_ref, w_hbm, o_ref, stage_ref, wb_ref, sem, *, layer, tn):
    j, i = pl.program_id(0), pl.program_id(1)

    def panel_copy(jj):
        return pltpu.make_async_copy(w_hbm.at[layer, :, pl.ds(pl.multiple_of(jj * tn, tn), tn)], stage_ref, sem)

    @pl.when((j == 0) & (i == 0))
    def _():
        panel_copy(0).start()

    @pl.when(i == 0)
    def _():
        panel_copy(j).wait()
        wb_ref[...] = stage_ref[...].astype(BF16)

        @pl.when(j + 1 < pl.num_programs(0))
        def _():
            panel_copy(j + 1).start()

    o_ref[...] = jnp.dot(a_ref[...], wb_ref[...], preferred_element_type=F32).astype(o_ref.dtype)


def matmul_f32w(a, w, layer, out_dtype, tm=1024, tn=1024):
    m, k = a.shape
    _, k2, n = w.shape
    assert k == k2
    tm, tn = _tile(m, tm), _tile(n, tn)
    return pl.pallas_call(
        functools.partial(_matmul_f32w_kernel, layer=layer, tn=tn),
        grid=(n // tn, m // tm),
        in_specs=[pl.BlockSpec((tm, k), lambda j, i: (i, 0)), pl.BlockSpec(memory_space=pl.ANY)],
        out_specs=pl.BlockSpec((tm, tn), lambda j, i: (i, j)),
        out_shape=jax.ShapeDtypeStruct((m, n), out_dtype),
        scratch_shapes=[pltpu.VMEM((k, tn), F32), pltpu.VMEM((k, tn), BF16), pltpu.SemaphoreType.DMA],
        compiler_params=_cparams(("arbitrary", "arbitrary"), 56),
        name="matmul_f32w",
    )(a, w)


def _layer_norm(y, g, b):
    mu = jnp.mean(y, axis=-1, keepdims=True)
    yc = y - mu
    var = jnp.mean(yc * yc, axis=-1, keepdims=True)
    return yc * lax.rsqrt(var + LN_EPS) * g + b


def _round_bf16_bits(v):
    return lax.bitcast_convert_type(v.astype(BF16).astype(F32), U32)


def _pack_rows(yn, o_ref, rows):
    half = yn.shape[1] // 2
    packed = (_round_bf16_bits(yn[:, :half]) >> 16) | (_round_bf16_bits(yn[:, half:]) & jnp.uint32(0xFFFF0000))
    for s in range(ROW_SLAB):
        o_ref[pl.ds(s, rows, stride=ROW_SLAB), :] = packed[:, s * LANES:(s + 1) * LANES]


def _unpack_rows(ref, base, rows, dtype):
    lo, hi = [], []
    for s in range(ROW_SLAB):
        u = ref[pl.ds(base + s, rows, stride=ROW_SLAB), :]
        lo.append(lax.bitcast_convert_type(u << 16, F32).astype(dtype))
        hi.append(lax.bitcast_convert_type(u & jnp.uint32(0xFFFF0000), F32).astype(dtype))
    return jnp.concatenate(lo, axis=1), jnp.concatenate(hi, axis=1)


def _res_ln_proj_kernel(x_ref, h_ref, g_ref, b_ref, wq_ref, of_ref, oq_ref, *, alpha):
    yn = _layer_norm(alpha * x_ref[...] + h_ref[...].astype(F32), g_ref[...], b_ref[...])
    of_ref[...] = yn
    oq_ref[...] = jnp.dot(yn.astype(BF16), wq_ref[...], preferred_element_type=F32).astype(oq_ref.dtype)


def residual_layer_norm_proj(x, h, g, b, wq, layer, alpha, tm=256):
    t, d = x.shape
    nq = wq.shape[2]
    tm = _tile(t, tm)
    row = pl.BlockSpec((tm, d), lambda i: (i, 0))
    vec = pl.BlockSpec((1, d), lambda i: (0, 0))
    return pl.pallas_call(
        functools.partial(_res_ln_proj_kernel, alpha=alpha),
        grid=(t // tm,),
        in_specs=[row, row, vec, vec, pl.BlockSpec((None, d, nq), lambda i: (layer, 0, 0))],
        out_specs=[row, pl.BlockSpec((tm, nq), lambda i: (i, 0))],
        out_shape=[jax.ShapeDtypeStruct((t, d), F32), jax.ShapeDtypeStruct((t, nq), BF16)],
        compiler_params=_cparams(("parallel",), 56),
        name="residual_layer_norm_proj",
    )(x, h, g.reshape(1, d), b.reshape(1, d), wq)


def _memory_attention(q_ref, kv_ref):
    width = q_ref.shape[1]
    head_dim = width // MEM_HEADS
    scale = float(head_dim) ** -0.5
    outs = []
    for h in range(MEM_HEADS):
        cols = slice(h * head_dim, (h + 1) * head_dim)
        sc = lax.dot_general(q_ref[:, cols], kv_ref[:, cols], _TRANS_B, preferred_element_type=F32) * scale
        p = jnp.exp(sc - jnp.max(sc, axis=-1, keepdims=True))
        den = jnp.sum(p, axis=-1, keepdims=True)
        vh = kv_ref[:, width + h * head_dim:width + (h + 1) * head_dim]
        outs.append((jnp.dot(p.astype(BF16), vh, preferred_element_type=F32) / den).astype(BF16))
    return jnp.concatenate(outs, axis=1)


def _mem_attn_proj_kernel(q_ref, kv_ref, wo_ref, h_ref):
    h_ref[...] = jnp.dot(_memory_attention(q_ref, kv_ref), wo_ref[...], preferred_element_type=F32).astype(h_ref.dtype)


def memory_attention_proj(q, kv, wo, layer, tm=512):
    t, width = q.shape
    bsz, m, w2 = kv.shape
    d = wo.shape[2]
    tm = _tile(t // bsz, tm)
    tiles_per_batch = t // bsz // tm
    assert w2 == 2 * width
    return pl.pallas_call(
        _mem_attn_proj_kernel,
        grid=(t // tm,),
        in_specs=[pl.BlockSpec((tm, width), lambda i: (i, 0)),
                  pl.BlockSpec((None, m, w2), lambda i: (i // tiles_per_batch, 0, 0)),
                  pl.BlockSpec((None, width, d), lambda i: (layer, 0, 0))],
        out_specs=pl.BlockSpec((tm, d), lambda i: (i, 0)),
        out_shape=jax.ShapeDtypeStruct((t, d), BF16),
        compiler_params=_cparams(("parallel",), 48),
        name="memory_attention_proj",
    )(q, kv, wo)


def _res_ln_route_kernel(x_ref, h_ref, g_ref, b_ref, wr_ref, of_ref, og_ref, lg_ref, *, alpha):
    yn = _layer_norm(alpha * x_ref[...] + h_ref[...].astype(F32), g_ref[...], b_ref[...])
    of_ref[...] = yn
    _pack_rows(yn, og_ref, yn.shape[0])
    y_hi = yn.astype(BF16)
    y_lo = (yn - y_hi.astype(F32)).astype(BF16)
    both = jnp.dot(y_hi, wr_ref[...], preferred_element_type=F32)
    lg_ref[...] = both[:, :LANES] + both[:, LANES:] + jnp.dot(y_lo, wr_ref[:, :LANES], preferred_element_type=F32)


def residual_layer_norm_route(x, h, g, b, router_w2, alpha, tm=256):
    t, d = x.shape
    tm = _tile(t, tm)
    assert d == 2 * ROW_SLAB * LANES
    row = pl.BlockSpec((tm, d), lambda i: (i, 0))
    vec = pl.BlockSpec((1, d), lambda i: (0, 0))
    return pl.pallas_call(
        functools.partial(_res_ln_route_kernel, alpha=alpha),
        grid=(t // tm,),
        in_specs=[row, row, vec, vec, pl.BlockSpec((d, 2 * LANES), lambda i: (0, 0))],
        out_specs=[row, pl.BlockSpec((tm * ROW_SLAB, LANES), lambda i: (i, 0)), pl.BlockSpec((tm, LANES), lambda i: (i, 0))],
        out_shape=[jax.ShapeDtypeStruct((t, d), F32), jax.ShapeDtypeStruct((t * ROW_SLAB, LANES), U32),
                   jax.ShapeDtypeStruct((t, LANES), F32)],
        compiler_params=_cparams(("parallel",), 48),
        name="residual_layer_norm_route",
    )(x, h, g.reshape(1, d), b.reshape(1, d), router_w2)


def _gla_kernel(q_ref, k_ref, v_ref, g_ref, gd_ref, wu_ref, bg_ref, ng_ref, o_ref, st_ref, *, n_groups, q_scale):
    rows_g = GLA_GROUP * CHUNK

    @pl.when(pl.program_id(2) == 0)
    def _():
        st_ref[...] = jnp.zeros_like(st_ref)

    ri = lax.broadcasted_iota(I32, (rows_g, rows_g), 0)
    ci = lax.broadcasted_iota(I32, (rows_g, rows_g), 1)
    chunk_gap = ri // CHUNK - ci // CHUNK
    causal = (chunk_gap == 0) & (ri >= ci)
    tri = causal.astype(BF16)

    def stack(rows_of_chunk):
        return jnp.concatenate([jnp.broadcast_to(r, (CHUNK, r.shape[1])) for r in rows_of_chunk], axis=0)

    def group(gi, carry):
        rows = pl.ds(pl.multiple_of(gi * rows_g, rows_g), rows_g)
        q = q_ref[rows, :].astype(F32) * q_scale
        k = k_ref[rows, :].astype(F32)
        v = v_ref[rows, :]
        gate = jnp.dot(gd_ref[rows, :].astype(BF16), wu_ref[...], preferred_element_type=F32) + bg_ref[...]
        la = (jnp.minimum(gate, 0.0) - jnp.log(1.0 + jnp.exp(-jnp.abs(gate)))) * (1.0 / GLA_GATE_TAU)
        hi = la.astype(BF16)
        r1 = la - hi.astype(F32)
        mid = r1.astype(BF16)
        lo = (r1 - mid.astype(F32)).astype(BF16)
        bc = (jnp.dot(tri, hi, preferred_element_type=F32) + jnp.dot(tri, mid, preferred_element_type=F32)
              + jnp.dot(tri, lo, preferred_element_type=F32))
        bl = [bc[(c + 1) * CHUNK - 1:(c + 1) * CHUNK, :] for c in range(GLA_GROUP)]
        zero = jnp.zeros_like(bl[0])
        before = [sum(bl[:c], zero) for c in range(GLA_GROUP)]
        after = [sum(bl[c + 1:], zero) for c in range(GLA_GROUP)]
        q_e = q * jnp.exp(bc)
        k_e = k * jnp.exp(-bc)
        k_d = k_e * stack([jnp.exp(r) for r in bl])
        k_eb = k_e.astype(BF16)
        k_db = k_d.astype(BF16)
        att = jnp.where(causal, lax.dot_general(q_e.astype(BF16), k_eb, _TRANS_B, preferred_element_type=F32), 0.0)
        for gap in range(1, GLA_GROUP):
            between = [jnp.exp(sum(bl[c - gap + 1:c], zero)) if c >= gap else zero for c in range(GLA_GROUP)]
            q_gap = q_e if gap == 1 else q_e * stack(between)
            att = att + jnp.where(chunk_gap == gap,
                                  lax.dot_general(q_gap.astype(BF16), k_db, _TRANS_B, preferred_element_type=F32), 0.0)
        st = st_ref[...]
        q_st = (q_e * stack([jnp.exp(r) for r in before])).astype(BF16)
        o = jnp.dot(att.astype(BF16), v, preferred_element_type=F32) + lax.dot_general(
            q_st, st.astype(BF16), _TRANS_B, preferred_element_type=F32)
        k_up = (k_d * stack([jnp.exp(r) for r in after])).astype(BF16)
        st_ref[...] = st * jnp.exp(before[-1] + bl[-1]) + lax.dot_general(v, k_up, _TRANS_A, preferred_element_type=F32)
        on = o * lax.rsqrt(jnp.mean(o * o, axis=-1, keepdims=True) + LN_EPS) * ng_ref[...]
        g = g_ref[rows, :].astype(F32)
        o_ref[rows, :] = (on * (g / (1.0 + jnp.exp(-g)))).astype(o_ref.dtype)
        return carry

    lax.fori_loop(0, n_groups, group, 0, unroll=2)


def gla_core(z, gd, wu, bg, ng, layer, rows_per_step=512):
    b, s, zc = z.shape
    hk = wu.shape[3]
    hv = ng.shape[3]
    assert zc == GLA_HEADS * (2 * hk + 2 * hv) and hv == 2 * hk
    rb = _tile(s, rows_per_step)
    assert rb % (2 * GLA_GROUP * CHUNK) == 0
    kq = GLA_HEADS
    kv = 2 * GLA_HEADS * hk // hv
    kg = kv + GLA_HEADS
    return pl.pallas_call(
        functools.partial(_gla_kernel, n_groups=rb // (GLA_GROUP * CHUNK), q_scale=float(hk) ** -0.5),
        grid=(b, GLA_HEADS, s // rb),
        in_specs=[
            pl.BlockSpec((None, rb, hk), lambda bi, h, r: (bi, r, h)),
            pl.BlockSpec((None, rb, hk), lambda bi, h, r: (bi, r, kq + h)),
            pl.BlockSpec((None, rb, hv), lambda bi, h, r: (bi, r, kv + h)),
            pl.BlockSpec((None, rb, hv), lambda bi, h, r: (bi, r, kg + h)),
            pl.BlockSpec((None, rb, LANES), lambda bi, h, r: (bi, r, 0)),
            pl.BlockSpec((None, None, LANES, hk), lambda bi, h, r: (layer, h, 0, 0)),
            pl.BlockSpec((None, None, 1, hk), lambda bi, h, r: (layer, h, 0, 0)),
            pl.BlockSpec((None, None, 1, hv), lambda bi, h, r: (layer, h, 0, 0)),
        ],
        out_specs=pl.BlockSpec((None, rb, hv), lambda bi, h, r: (bi, r, h)),
        out_shape=jax.ShapeDtypeStruct((b, s, GLA_HEADS * hv), BF16),
        scratch_shapes=[pltpu.VMEM((hv, hk), F32)],
        compiler_params=_cparams(("parallel", "parallel", "arbitrary"), 40),
        name="gla_core",
    )(z, z, z, z, gd, wu, bg, ng)


def _band_attn_kernel(q_ref, k_ref, v_ref, bias_ref, o_ref, kp_ref, vp_ref, *, n_blocks, scale):
    pad = ATT_LEFT_CHUNKS * CHUNK
    s = n_blocks * ATT_QB
    kp_ref[pl.ds(0, pad), :] = jnp.zeros((pad, ATT_HEAD_DIM), kp_ref.dtype)
    vp_ref[pl.ds(0, pad), :] = jnp.zeros((pad, ATT_HEAD_DIM), vp_ref.dtype)
    kp_ref[pl.ds(pad, s), :] = k_ref[...]
    vp_ref[pl.ds(pad, s), :] = v_ref[...]
    last_tile = bias_ref.shape[0] - 1

    def block(i, carry):
        start = pl.multiple_of(i * ATT_QB, ATT_QB)
        q = (q_ref[pl.ds(start, ATT_QB), :].astype(F32) * scale).astype(BF16)
        kb = kp_ref[pl.ds(start, ATT_WIN), :]
        vb = vp_ref[pl.ds(start, ATT_WIN), :]
        sc = lax.dot_general(q, kb, _TRANS_B, preferred_element_type=F32) + bias_ref[jnp.minimum(i, last_tile)]
        p = jnp.exp(sc - jnp.max(sc, axis=-1, keepdims=True))
        den = jnp.sum(p, axis=-1, keepdims=True)
        o = jnp.dot(p.astype(BF16), vb, preferred_element_type=F32) / den
        o_ref[pl.ds(start, ATT_QB), :] = o.astype(o_ref.dtype)
        return carry

    lax.fori_loop(0, n_blocks, block, 0, unroll=ATT_UNROLL)


def band_attention_core(qkv, bias):
    b, s, c3 = qkv.shape
    h, n_tiles = bias.shape[:2]
    assert c3 == 3 * h * ATT_HEAD_DIM and s % (ATT_UNROLL * ATT_QB) == 0
    blk = lambda off: pl.BlockSpec((None, s, ATT_HEAD_DIM), lambda bi, hi: (bi, 0, off + hi))
    return pl.pallas_call(
        functools.partial(_band_attn_kernel, n_blocks=s // ATT_QB, scale=float(ATT_HEAD_DIM) ** -0.5),
        grid=(b, h),
        in_specs=[blk(0), blk(h), blk(2 * h),
                  pl.BlockSpec((None, n_tiles, ATT_QB, ATT_WIN), lambda bi, hi: (hi, 0, 0, 0))],
        out_specs=blk(0),
        out_shape=jax.ShapeDtypeStruct((b, s, h * ATT_HEAD_DIM), BF16),
        scratch_shapes=[pltpu.VMEM((s + ATT_LEFT_CHUNKS * CHUNK, ATT_HEAD_DIM), BF16)] * 2,
        compiler_params=_cparams(("parallel", "parallel"), 40),
        name="band_attention_core",
    )(qkv, qkv, qkv, bias)


def _rel_bias(rel_table):
    h = rel_table.shape[0]
    pad = ATT_LEFT_CHUNKS * CHUNK
    assert ATT_REL_CLIP >= CHUNK and 2 * ATT_REL_CLIP + 1 >= ATT_BAND - ATT_REL_CLIP
    rev = rel_table[:, ::-1].astype(F32)
    lead = pad - ATT_REL_CLIP + CHUNK - 1
    ext = jnp.concatenate([jnp.broadcast_to(rev[:, :1], (h, lead)), rev[:, :ATT_BAND - ATT_REL_CLIP]], axis=1)
    band = jnp.stack([ext[:, CHUNK - 1 - i:CHUNK - 1 - i + ATT_BAND] for i in range(CHUNK)], axis=1)
    neg = jnp.full((h, CHUNK, ATT_WIN), -jnp.inf, F32)
    window = jnp.concatenate(
        [lax.dynamic_update_slice(neg, band, (0, 0, c * CHUNK)) for c in range(ATT_QUERY_CHUNKS)], axis=1)
    n_tiles = -(-pad // ATT_QB) + 1
    slot = np.arange(ATT_WIN)[None, :]
    keep = np.stack([np.broadcast_to(slot >= pad - n * ATT_QB, (ATT_QB, ATT_WIN)) for n in range(n_tiles)])
    return jnp.where(keep[None], window[:, None], -jnp.inf)


def _top2_of4(a, b, c, d):
    hi1, lo1 = jnp.maximum(a, b), jnp.minimum(a, b)
    hi2, lo2 = jnp.maximum(c, d), jnp.minimum(c, d)
    return jnp.maximum(hi1, hi2) + jnp.maximum(jnp.minimum(hi1, hi2), jnp.maximum(lo1, lo2))


def _argmax_first(vals):
    best, idx = vals[0], jnp.zeros(vals[0].shape, I32)
    for j in range(1, len(vals)):
        better = vals[j] > best
        best = jnp.where(better, vals[j], best)
        idx = jnp.where(better, j, idx)
    return best, idx


def _router_kernel(lg_ref, br_ref, route_ref, wts_ref, cnt_ref, upper_ref, *, n_experts):
    tr = lg_ref.shape[0]
    per_group = n_experts // MOE_GROUPS

    @pl.when(pl.program_id(0) == 0)
    def _():
        cnt_ref[...] = jnp.zeros_like(cnt_ref)
        earlier = lax.broadcasted_iota(I32, (tr, tr), 0) < lax.broadcasted_iota(I32, (tr, tr), 1)
        upper_ref[...] = earlier.astype(BF16)

    lt = lg_ref[...].T[:n_experts, :] + br_ref[:n_experts, :]
    ex = jnp.exp(lt - jnp.max(lt, axis=0, keepdims=True))
    probs = ex / jnp.sum(ex, axis=0, keepdims=True)
    p = [probs[e:e + 1, :] for e in range(n_experts)]

    assert per_group == 4
    scores = [_top2_of4(*p[per_group * g:per_group * (g + 1)]) for g in range(MOE_GROUPS)]
    _, best_group = _argmax_first(scores)
    in_group = [sum(jnp.where(best_group == g, p[per_group * g + j], 0.0) for g in range(MOE_GROUPS))
                for j in range(per_group)]
    p1, j1 = _argmax_first(in_group)
    p2, j2 = _argmax_first([jnp.where(j1 == j, -1.0, in_group[j]) for j in range(per_group)])
    e1 = best_group * per_group + j1
    e2 = best_group * per_group + j2
    w1 = p1 / (p1 + p2)
    w2 = p2 / (p1 + p2)

    erow = lax.broadcasted_iota(I32, (n_experts, tr), 0)
    hit1 = erow == e1
    hit2 = erow == e2
    onehot = hit1.astype(F32) + hit2.astype(F32)
    before = cnt_ref[:, 0:1] + jnp.dot(onehot.astype(BF16), upper_ref[...], preferred_element_type=F32)
    rank1 = jnp.sum(jnp.where(hit1, before, 0.0), axis=0, keepdims=True).astype(I32)
    rank2 = jnp.sum(jnp.where(hit2, before, 0.0), axis=0, keepdims=True).astype(I32)
    cnt_ref[...] += jnp.sum(onehot, axis=1, keepdims=True)

    rrow = lax.broadcasted_iota(I32, (SUBLANES, tr), 0)
    route_ref[...] = jnp.where(rrow == 0, e1, jnp.where(rrow == 1, e2, jnp.where(rrow == 2, rank1, rank2)))
    wrow = lax.broadcasted_iota(I32, (LANES, tr), 0)
    wts_ref[...] = jnp.where(wrow == 0, w1, jnp.where(wrow == 1, w2, 0.0)).T


def moe_router(logits, br, n_experts, tr=1024):
    t = logits.shape[0]
    tr = _tile(t, tr)
    return pl.pallas_call(
        functools.partial(_router_kernel, n_experts=n_experts),
        grid=(t // tr,),
        in_specs=[pl.BlockSpec((tr, LANES), lambda i: (i, 0)), pl.BlockSpec((LANES, 1), lambda i: (0, 0))],
        out_specs=[
            pl.BlockSpec((SUBLANES, tr), lambda i: (0, i)),
            pl.BlockSpec((tr, LANES), lambda i: (i, 0)),
            pl.BlockSpec((n_experts, LANES), lambda i: (0, 0)),
        ],
        out_shape=[
            jax.ShapeDtypeStruct((SUBLANES, t), I32),
            jax.ShapeDtypeStruct((t, LANES), F32),
            jax.ShapeDtypeStruct((n_experts, LANES), F32),
        ],
        scratch_shapes=[pltpu.VMEM((tr, tr), BF16)],
        compiler_params=_cparams(("arbitrary",), 32),
        name="moe_router",
    )(logits, br)


def _row_copy(src_ref, src_row, dst_ref, dst_row, sem):
    return pltpu.make_async_copy(
        src_ref.at[pl.ds(pl.multiple_of(src_row * ROW_SLAB, ROW_SLAB), ROW_SLAB), :],
        dst_ref.at[pl.ds(pl.multiple_of(dst_row * ROW_SLAB, ROW_SLAB), ROW_SLAB), :],
        sem)


def _rows_wait(src_ref, dst_ref, n_rows, sem):
    span = pl.ds(0, n_rows * ROW_SLAB)
    pltpu.make_async_copy(src_ref.at[span, :], dst_ref.at[span, :], sem).wait()


def _dispatch_kernel(pad_start_ref, pad_count_ref, pos_ref, xg_ref, xs_ref, zero_ref, sem, pad_sem, *, td, n_pads):
    @pl.when(pl.program_id(0) == 0)
    def _():
        zero_ref[...] = jnp.zeros_like(zero_ref)
        for s in range(n_pads):
            def fill(r, carry, s=s):
                _row_copy(zero_ref, 0, xs_ref, pad_start_ref[s] + r, pad_sem).start()
                return carry

            lax.fori_loop(0, pad_count_ref[s], fill, 0)
        for s in range(n_pads):
            def drain(r, carry):
                _row_copy(zero_ref, 0, xs_ref, 0, pad_sem).wait()
                return carry

            lax.fori_loop(0, pad_count_ref[s], drain, 0)

    def issue(t, carry):
        for slot in range(2):
            _row_copy(xg_ref, t, xs_ref, pos_ref[0, 0, slot * td + t], sem).start(priority=slot)
        return carry

    lax.fori_loop(0, td, issue, 0, unroll=4)
    for _ in range(2):
        _rows_wait(xg_ref, xs_ref, td, sem)


def moe_dispatch(xg, pos_blocks, pad_start, pad_count, n_sorted_rows, td):
    t = xg.shape[0] // ROW_SLAB
    return pl.pallas_call(
        functools.partial(_dispatch_kernel, td=td, n_pads=pad_start.shape[0]),
        grid_spec=pltpu.PrefetchScalarGridSpec(
            num_scalar_prefetch=2,
            grid=(t // td,),
            in_specs=[
                pl.BlockSpec((1, 1, 2 * td), lambda i, ps, pc: (i, 0, 0), memory_space=pltpu.SMEM),
                pl.BlockSpec((td * ROW_SLAB, LANES), lambda i, ps, pc: (i, 0)),
            ],
            out_specs=pl.BlockSpec(memory_space=pl.ANY),
            scratch_shapes=[pltpu.VMEM((ROW_SLAB, LANES), U32), pltpu.SemaphoreType.DMA, pltpu.SemaphoreType.DMA],
        ),
        out_shape=jax.ShapeDtypeStruct((n_sorted_rows * ROW_SLAB, LANES), U32),
        compiler_params=_cparams(("arbitrary",), 32),
        name="moe_dispatch",
    )(pad_start, pad_count, pos_blocks, xg)


def _expert_kernel(te_ref, nu_ref, first_ref, next_ref, xs_ref, wgu_hbm, wd_hbm, y_ref,
                   gu_f32_ref, dn_f32_ref, wgu_ref, wd_ref, sem, *, tm, layer):
    i = pl.program_id(0)
    used = i < nu_ref[0]

    def weight_copies(e):
        return (pltpu.make_async_copy(wgu_hbm.at[layer, e], gu_f32_ref, sem.at[0]),
                pltpu.make_async_copy(wd_hbm.at[layer, e], dn_f32_ref, sem.at[1]))

    @pl.when(i == 0)
    def _():
        for c in weight_copies(te_ref[0]):
            c.start()

    @pl.when(used & (first_ref[i] == 1))
    def _():
        for c in weight_copies(te_ref[i]):
            c.wait()
        wgu_ref[...] = gu_f32_ref[...].astype(BF16)
        wd_ref[...] = dn_f32_ref[...].astype(BF16)

        @pl.when(next_ref[i] >= 0)
        def _():
            for c in weight_copies(next_ref[i]):
                c.start()

    @pl.when(used)
    def _():
        half = wgu_ref.shape[0] // 2
        ff = wd_ref.shape[0]
        x_lo, x_hi = _unpack_rows(xs_ref, 0, tm, BF16)
        gu = (jnp.dot(x_lo, wgu_ref[:half, :], preferred_element_type=F32)
              + jnp.dot(x_hi, wgu_ref[half:, :], preferred_element_type=F32))
        gate, up = gu[:, :ff], gu[:, ff:]
        hid = (gate / (1.0 + jnp.exp(-gate)) * up).astype(BF16)
        _pack_rows(jnp.dot(hid, wd_ref[...], preferred_element_type=F32), y_ref, tm)

    @pl.when(jnp.logical_not(used))
    def _():
        y_ref[...] = jnp.zeros_like(y_ref)


def moe_experts(xs, wgu, wd, layer, tile_expert, n_used, seg_first, seg_next, tm):
    n_tiles = xs.shape[0] // (tm * ROW_SLAB)
    _, _, d, ff2 = wgu.shape
    rows = pl.BlockSpec((tm * ROW_SLAB, LANES), lambda i, *_: (i, 0))
    return pl.pallas_call(
        functools.partial(_expert_kernel, tm=tm, layer=layer),
        grid_spec=pltpu.PrefetchScalarGridSpec(
            num_scalar_prefetch=4,
            grid=(n_tiles,),
            in_specs=[rows, pl.BlockSpec(memory_space=pl.ANY), pl.BlockSpec(memory_space=pl.ANY)],
            out_specs=rows,
            scratch_shapes=[pltpu.VMEM((d, ff2), F32), pltpu.VMEM((ff2 // 2, d), F32),
                            pltpu.VMEM((d, ff2), BF16), pltpu.VMEM((ff2 // 2, d), BF16),
                            pltpu.SemaphoreType.DMA((2,))],
        ),
        out_shape=jax.ShapeDtypeStruct(xs.shape, U32),
        compiler_params=_cparams(("arbitrary",), 56),
        name="moe_experts",
    )(tile_expert, n_used, seg_first, seg_next, xs, wgu, wd)


def _combine_ln_kernel(pos_ref, pos_next_ref, wts_ref, x_ref, ys_ref, g_ref, b_ref, of_ref, ob_ref, buf_ref, sem,
                       *, tc, alpha):
    i = pl.program_id(0)
    cur = i % 2

    def start_gather(p_ref, b):
        def issue(t, carry):
            for slot in range(2):
                _row_copy(ys_ref, p_ref[0, 0, slot * tc + t], buf_ref.at[b], slot * tc + t, sem.at[b]).start()
            return carry

        lax.fori_loop(0, tc, issue, 0, unroll=4)

    @pl.when(i == 0)
    def _():
        start_gather(pos_ref, 0)

    @pl.when(i + 1 < pl.num_programs(0))
    def _():
        start_gather(pos_next_ref, 1 - cur)

    rows_in = buf_ref.at[cur]
    _rows_wait(ys_ref, rows_in, 2 * tc, sem.at[cur])

    def group(gi, carry):
        r0 = pl.multiple_of(gi * COMBINE_ROWS, COMBINE_ROWS)
        rows = pl.ds(r0, COMBINE_ROWS)
        y = alpha * x_ref[rows, :]
        for slot in range(2):
            lo, hi = _unpack_rows(rows_in, (slot * tc + r0) * ROW_SLAB, COMBINE_ROWS, F32)
            y = y + wts_ref[rows, slot:slot + 1] * jnp.concatenate([lo, hi], axis=1)
        yn = _layer_norm(y, g_ref[...], b_ref[...])
        of_ref[rows, :] = yn
        ob_ref[rows, :] = yn.astype(BF16)
        return carry

    lax.fori_loop(0, tc // COMBINE_ROWS, group, 0, unroll=2)


def moe_combine_layer_norm(x, ys, pos_blocks, wts, g, b, alpha, tc):
    t, d = x.shape
    n = t // tc
    row = pl.BlockSpec((tc, d), lambda i: (i, 0))
    vec = pl.BlockSpec((1, d), lambda i: (0, 0))
    return pl.pallas_call(
        functools.partial(_combine_ln_kernel, tc=tc, alpha=alpha),
        grid=(n,),
        in_specs=[
            pl.BlockSpec((1, 1, 2 * tc), lambda i: (i, 0, 0), memory_space=pltpu.SMEM),
            pl.BlockSpec((1, 1, 2 * tc), lambda i: (jnp.minimum(i + 1, n - 1), 0, 0), memory_space=pltpu.SMEM),
            pl.BlockSpec((tc, LANES), lambda i: (i, 0)),
            row,
            pl.BlockSpec(memory_space=pl.ANY),
            vec,
            vec,
        ],
        out_specs=[row, row],
        out_shape=[jax.ShapeDtypeStruct((t, d), F32), jax.ShapeDtypeStruct((t, d), BF16)],
        scratch_shapes=[pltpu.VMEM((2, 2 * tc * ROW_SLAB, LANES), U32), pltpu.SemaphoreType.DMA((2,))],
        compiler_params=_cparams(("arbitrary",), 48),
        name="moe_combine_layer_norm",
    )(pos_blocks, pos_blocks, wts, x, ys, g.reshape(1, d), b.reshape(1, d))


def _pos_blocks(pos, tile):
    t = pos.shape[1]
    return pos.reshape(2, t // tile, tile).transpose(1, 0, 2).reshape(t // tile, 1, 2 * tile)


def grouped_moe_layer_norm(xf, xg, logits, br, wgu, wd, layer, g, b, alpha, tm=256, td=512, tc=256):
    t, d = xf.shape
    n_experts = wgu.shape[1]
    route, wts, counts = moe_router(logits, br, n_experts)
    cnt = counts[:, 0].astype(I32)
    padded = (cnt + tm - 1) // tm * tm
    ends = jnp.cumsum(padded)
    offs = ends - padded
    expert_ids = jnp.arange(n_experts, dtype=I32)
    pos = jnp.sum(jnp.where(route[0:2, :, None] == expert_ids, offs, 0), axis=-1) + route[2:4]
    n_tiles = (2 * t) // tm + n_experts
    tile_starts = jnp.arange(n_tiles, dtype=I32) * tm
    tile_expert = jnp.minimum(jnp.sum((ends[None, :] <= tile_starts[:, None]).astype(I32), axis=1), n_experts - 1)
    n_used = (ends[-1:] // tm).astype(I32)
    tile_ids = jnp.arange(n_tiles, dtype=I32)
    of_tile = lambda table: jnp.sum(jnp.where(tile_expert[:, None] == expert_ids, table, 0), axis=1)
    seg_first = ((tile_ids == of_tile(offs // tm)) & (tile_ids < n_used)).astype(I32)
    next_start = of_tile(ends // tm)
    expert_at = jnp.sum(jnp.where(next_start[:, None] == tile_ids, tile_expert, 0), axis=1)
    seg_next = jnp.where(next_start < n_used, expert_at, -1).astype(I32)
    pad_start = jnp.concatenate([offs + cnt, ends[-1:]])
    pad_count = jnp.concatenate([padded - cnt, n_tiles * tm - ends[-1:]])
    xs = moe_dispatch(xg, _pos_blocks(pos, td), pad_start, pad_count, n_tiles * tm, td)
    ys = moe_experts(xs, wgu, wd, layer, tile_expert, n_used, seg_first, seg_next, tm)
    return moe_combine_layer_norm(xf, ys, _pos_blocks(pos, tc), wts, g, b, alpha, tc)


def kernel(x, mem, gla_w_in, gla_w_gate_down, gla_w_gate_up, gla_b_gate, gla_norm_g, gla_w_o, att_w_qkv, att_rel_bias, att_w_o, mx_w_q, mx_w_kv, mx_w_o, router_w, router_b, moe_w_gu, moe_w_down, ln_g, ln_b):
    bsz, seq, d = x.shape
    t = bsz * seq
    depth = ln_g.shape[0]
    alpha = (2.0 * depth) ** 0.25
    n_experts = router_w.shape[1]
    n_gla = gla_w_in.shape[0]
    hk = gla_w_gate_up.shape[2] // GLA_HEADS
    hv = gla_norm_g.shape[1] // GLA_HEADS
    rank = gla_w_gate_down.shape[2]
    mem_len = mem.shape[1]

    xf = x.reshape(t, d)
    xb = xf.astype(BF16)
    mem_b = mem.reshape(bsz * mem_len, d).astype(BF16)
    wr = jnp.pad(router_w, ((0, 0), (0, LANES - n_experts)))
    wr_hi = wr.astype(BF16)
    wr2 = jnp.concatenate([wr_hi, (wr - wr_hi.astype(F32)).astype(BF16)], axis=1)
    br = jnp.pad(router_b, (0, LANES - n_experts)).reshape(LANES, 1)
    w_mq, w_mo = mx_w_q.astype(BF16), mx_w_o.astype(BF16)
    w_gd = jnp.pad(gla_w_gate_down, ((0, 0), (0, 0), (0, LANES - rank))).astype(BF16)
    w_gup = jnp.pad(gla_w_gate_up.reshape(n_gla, rank, GLA_HEADS, hk).transpose(0, 2, 1, 3),
                    ((0, 0), (0, 0), (0, LANES - rank), (0, 0))).astype(BF16)
    b_gate = gla_b_gate.reshape(n_gla, GLA_HEADS, 1, hk)
    n_gain = gla_norm_g.reshape(n_gla, GLA_HEADS, 1, hv)

    for i in range(depth):
        j = i // 2
        if i % 2 == 0:
            z = matmul_f32w(xb, gla_w_in, j, BF16)
            gd = matmul(xb, w_gd, j, F32)
            o = gla_core(z.reshape(bsz, seq, -1), gd.reshape(bsz, seq, LANES), w_gup, b_gate, n_gain, j)
            h = matmul_f32w(o.reshape(t, -1), gla_w_o, j, BF16)
        else:
            qkv = matmul_f32w(xb, att_w_qkv, j, BF16)
            o = band_attention_core(qkv.reshape(bsz, seq, -1), _rel_bias(att_rel_bias[j]))
            h = matmul_f32w(o.reshape(t, -1), att_w_o, j, BF16)
        xf, q = residual_layer_norm_proj(xf, h, ln_g[i, 0], ln_b[i, 0], w_mq, i, alpha)

        kv = matmul_f32w(mem_b, mx_w_kv, i, BF16)
        h = memory_attention_proj(q, kv.reshape(bsz, mem_len, -1), w_mo, i)
        xf, xg, logits = residual_layer_norm_route(xf, h, ln_g[i, 1], ln_b[i, 1], wr2, alpha)

        xf, xb = grouped_moe_layer_norm(xf, xg, logits, br, moe_w_gu, moe_w_down, i, ln_g[i, 2], ln_b[i, 2], alpha)
    return xf.reshape(bsz, seq, d)
```
